```python
import math
import jax
import jax.numpy as jnp
from jax import lax
import numpy as np

D_MODEL = 1024
BATCH = 32
SEQ = 2048
DEPTH = 1

NSA_HEADS = 16
NSA_GROUPS = 4
NSA_HPG = NSA_HEADS // NSA_GROUPS
NSA_DK = 64
NSA_DV = 64
CMP_LEN = 32
CMP_STRIDE = 16
CMP_HIDDEN = 256
SEL_BLOCK = 64
SEL_TOPK = 16
WINDOW = 512
MLA_HEADS = 8
MLA_Q_RANK = 256
MLA_KV_RANK = 128
MLA_NOPE = 64
MLA_ROPE = 32
MLA_V = 128
ROPE_THETA = 10000.0
REL_BUCKETS = 32
REL_MAX_DIST = 128
D_FF = 2816
CONV_WIDTH = 3
Q_BLOCK = 128
RMS_EPS = 1e-6
NEG_INF = -1e30

IN_SPLITS = (
    ("nsa_q", NSA_HEADS * NSA_DK),
    ("k_cmp", NSA_GROUPS * NSA_DK),
    ("v_cmp", NSA_GROUPS * NSA_DV),
    ("k_slc", NSA_GROUPS * NSA_DK),
    ("v_slc", NSA_GROUPS * NSA_DV),
    ("k_win", NSA_GROUPS * NSA_DK),
    ("v_win", NSA_GROUPS * NSA_DV),
    ("nsa_gate", NSA_HEADS * 3),
    ("mla_cq", MLA_Q_RANK),
    ("mla_ckv", MLA_KV_RANK),
    ("mla_krope", MLA_ROPE),
    ("merge_a", D_MODEL),
    ("merge_b", D_MODEL),
)
IN_COLS = sum(w for _, w in IN_SPLITS)

kernel_name = "hybrid_nsa_mla_convglu_block"


def rmsnorm(x, g):
    xf = x.astype(jnp.float32)
    y = xf * lax.rsqrt(jnp.mean(xf * xf, axis=-1, keepdims=True) + RMS_EPS)
    return (y * g.astype(jnp.float32)).astype(x.dtype)


def split_columns(z):
    parts, off = {}, 0
    for name, width in IN_SPLITS:
        parts[name] = z[..., off:off + width]
        off += width
    return parts


def t5_bucket(dist):
    n = jnp.maximum(dist, 0)
    exact = REL_BUCKETS // 2
    log_ratio = jnp.log(jnp.maximum(n, exact).astype(jnp.float32) / exact) / math.log(REL_MAX_DIST / exact)
    large = jnp.minimum(exact + (log_ratio * (REL_BUCKETS - exact)).astype(jnp.int32), REL_BUCKETS - 1)
    return jnp.where(n < exact, n, large)


def compress_blocks(kv, pos_emb, w1, w2):
    B, S, G, d = kv.shape
    nc = (S - CMP_LEN) // CMP_STRIDE + 1
    idx = np.arange(nc)[:, None] * CMP_STRIDE + np.arange(CMP_LEN)[None, :]
    blocks = kv[:, idx] + pos_emb[None, None, :, None, :]
    blocks = blocks.transpose(0, 1, 3, 2, 4).reshape(B, nc, G, CMP_LEN * d)
    return jax.nn.gelu(blocks @ w1) @ w2


def nsa_one(q, kc, vc, ks, vs, kw, vw, gates, rel_table):
    S = q.shape[0]
    nc = kc.shape[0]
    nb = S // SEL_BLOCK
    n_sel = min(SEL_TOPK, nb)
    G, HPG = NSA_GROUPS, NSA_HPG
    scale = NSA_DK ** -0.5
    t = jnp.arange(S)
    qg = q.reshape(S, G, HPG, NSA_DK)
    rel_g = rel_table.reshape(REL_BUCKETS, G, HPG)

    dist_c = t[:, None] - (jnp.arange(nc) * CMP_STRIDE + CMP_LEN - 1)[None, :]
    valid_c = dist_c >= 0
    bias_c = rel_g[t5_bucket(dist_c)].transpose(2, 3, 0, 1)
    logit_c = jnp.einsum("sghd,cgd->ghsc", qg, kc).astype(jnp.float32) * scale + bias_c
    p_c = jax.nn.softmax(jnp.where(valid_c, logit_c, NEG_INF), axis=-1) * valid_c
    o_cmp = jnp.einsum("ghsc,cgd->sghd", p_c.astype(vc.dtype), vc).reshape(S, NSA_HEADS, NSA_DV)

    cs = np.arange(nc) * CMP_STRIDE
    bs = np.arange(nb) * SEL_BLOCK
    overlap = np.clip(np.minimum(cs[:, None] + CMP_LEN, bs[None, :] + SEL_BLOCK)
                      - np.maximum(cs[:, None], bs[None, :]), 0, None) / CMP_LEN
    score = jnp.einsum("ghsc,cj->gsj", p_c, jnp.asarray(overlap, dtype=jnp.float32))
    cur = (t // SEL_BLOCK)[:, None]
    j = jnp.arange(nb)[None, :]
    forced = (j == 0) | (j == cur) | (j == cur - 1)
    score = jnp.where(forced, jnp.inf, jnp.where(j > cur, -jnp.inf, score))
    sel_idx = lax.top_k(score, n_sel)[1]
    ks_blk = ks.reshape(nb, SEL_BLOCK, G, NSA_DK).transpose(2, 0, 1, 3)
    vs_blk = vs.reshape(nb, SEL_BLOCK, G, NSA_DV).transpose(2, 0, 1, 3)
    kw_pad = jnp.pad(kw, ((WINDOW, 0), (0, 0), (0, 0)))
    vw_pad = jnp.pad(vw, ((WINDOW, 0), (0, 0), (0, 0)))
    g_ix = jnp.arange(G)[:, None, None]
    n_win = WINDOW + Q_BLOCK

    def query_block(i):
        s0 = i * Q_BLOCK
        tq = s0 + jnp.arange(Q_BLOCK)
        qb = lax.dynamic_slice_in_dim(qg, s0, Q_BLOCK, 0)
        idx = lax.dynamic_slice_in_dim(sel_idx, s0, Q_BLOCK, 1)
        k_sel = ks_blk[g_ix, idx]
        v_sel = vs_blk[g_ix, idx]
        dist_s = tq[None, :, None, None] - (idx[..., None] * SEL_BLOCK + jnp.arange(SEL_BLOCK))
        bias_s = rel_g[t5_bucket(dist_s), g_ix[..., None]].transpose(0, 4, 1, 2, 3)
        logit_s = jnp.einsum("qghd,gqnkd->ghqnk", qb, k_sel).astype(jnp.float32) * scale + bias_s
        logit_s = jnp.where((dist_s >= 0)[:, None], logit_s, NEG_INF)
        p_s = jax.nn.softmax(logit_s.reshape(G, HPG, Q_BLOCK, n_sel * SEL_BLOCK), axis=-1)
        p_s = p_s.reshape(G, HPG, Q_BLOCK, n_sel, SEL_BLOCK)
        o_s = jnp.einsum("ghqnk,gqnkd->qghd", p_s.astype(v_sel.dtype), v_sel)
        k_w = lax.dynamic_slice_in_dim(kw_pad, s0, n_win, 0)
        v_w = lax.dynamic_slice_in_dim(vw_pad, s0, n_win, 0)
        kpos_w = s0 - WINDOW + jnp.arange(n_win)
        dist_w = tq[:, None] - kpos_w[None, :]
        valid_w = (dist_w >= 0) & (dist_w < WINDOW) & (kpos_w >= 0)[None, :]
        bias_w = rel_g[t5_bucket(dist_w)].transpose(2, 3, 0, 1)
        logit_w = jnp.einsum("qghd,kgd->ghqk", qb, k_w).astype(jnp.float32) * scale + bias_w
        p_w = jax.nn.softmax(jnp.where(valid_w, logit_w, NEG_INF), axis=-1)
        o_w = jnp.einsum("ghqk,kgd->qghd", p_w.astype(v_w.dtype), v_w)
        return o_s, o_w

    o_slc, o_win = lax.map(query_block, jnp.arange(S // Q_BLOCK))
    o_slc = o_slc.reshape(S, NSA_HEADS, NSA_DV)
    o_win = o_win.reshape(S, NSA_HEADS, NSA_DV)
    g = jax.nn.sigmoid(gates)
    out = g[..., 0:1] * o_cmp + g[..., 1:2] * o_slc + g[..., 2:3] * o_win
    return out.reshape(S, NSA_HEADS * NSA_DV)


def rope(x, cos, sin):
    half = x.shape[-1] // 2
    x1, x2 = x[..., :half], x[..., half:]
    return jnp.concatenate([x1 * cos - x2 * sin, x2 * cos + x1 * sin], axis=-1).astype(x.dtype)


def mla_attention(c_q, c_kv, k_rope, positions, q_norm_g, w_uq, kv_norm_g, w_ukv):
    B, S, _ = c_q.shape
    q = (rmsnorm(c_q, q_norm_g) @ w_uq).reshape(B, S, MLA_HEADS, MLA_NOPE + MLA_ROPE)
    kv = (rmsnorm(c_kv, kv_norm_g) @ w_ukv).reshape(B, S, MLA_HEADS, MLA_NOPE + MLA_V)
    q_nope, q_rope = q[..., :MLA_NOPE], q[..., MLA_NOPE:]
    k_nope, v = kv[..., :MLA_NOPE], kv[..., MLA_NOPE:]
    inv_freq = ROPE_THETA ** (-jnp.arange(0, MLA_ROPE, 2, dtype=jnp.float32) / MLA_ROPE)
    ang = positions.astype(jnp.float32)[..., None] * inv_freq
    cos, sin = jnp.cos(ang), jnp.sin(ang)
    q_rope = rope(q_rope, cos[:, :, None], sin[:, :, None])
    k_rope = rope(k_rope, cos, sin)
    scale = (MLA_NOPE + MLA_ROPE) ** -0.5
    k_pos = jnp.arange(S)

    def query_block(i):
        s0 = i * Q_BLOCK
        qn = lax.dynamic_slice_in_dim(q_nope, s0, Q_BLOCK, 1)
        qr = lax.dynamic_slice_in_dim(q_rope, s0, Q_BLOCK, 1)
        logit = (jnp.einsum("bqhd,bkhd->bhqk", qn, k_nope)
                 + jnp.einsum("bqhd,bkd->bhqk", qr, k_rope)).astype(jnp.float32) * scale
        causal = (s0 + jnp.arange(Q_BLOCK))[:, None] >= k_pos[None, :]
        p = jax.nn.softmax(jnp.where(causal, logit, NEG_INF), axis=-1)
        return jnp.einsum("bhqk,bkhd->bqhd", p.astype(v.dtype), v)

    o = lax.map(query_block, jnp.arange(S // Q_BLOCK))
    return o.transpose(1, 0, 2, 3, 4).reshape(B, S, MLA_HEADS * MLA_V)


def hybrid_mixer(h, positions, rel_table, w_in, cmp_pos_k, cmp_w1_k, cmp_w2_k,
                 cmp_pos_v, cmp_w1_v, cmp_w2_v, mla_q_norm_g, mla_w_uq,
                 mla_kv_norm_g, mla_w_ukv, w_o):
    B, S, _ = h.shape
    G = NSA_GROUPS
    p = split_columns(h @ w_in)
    q = p["nsa_q"].reshape(B, S, NSA_HEADS, NSA_DK)
    kc = compress_blocks(p["k_cmp"].reshape(B, S, G, NSA_DK), cmp_pos_k, cmp_w1_k, cmp_w2_k)
    vc = compress_blocks(p["v_cmp"].reshape(B, S, G, NSA_DV), cmp_pos_v, cmp_w1_v, cmp_w2_v)
    ks = p["k_slc"].reshape(B, S, G, NSA_DK)
    vs = p["v_slc"].reshape(B, S, G, NSA_DV)
    kw = p["k_win"].reshape(B, S, G, NSA_DK)
    vw = p["v_win"].reshape(B, S, G, NSA_DV)
    gates = p["nsa_gate"].reshape(B, S, NSA_HEADS, 3)
    o_nsa = lax.map(lambda a: nsa_one(*a, rel_table), (q, kc, vc, ks, vs, kw, vw, gates))
    o_mla = mla_attention(p["mla_cq"], p["mla_ckv"], p["mla_krope"], positions,
                          mla_q_norm_g, mla_w_uq, mla_kv_norm_g, mla_w_ukv)
    y = jax.nn.sigmoid(p["merge_a"]) * o_nsa + jax.nn.sigmoid(p["merge_b"]) * o_mla
    return y @ w_o


def conv_glu_ffn(h, w_gate, w_up, conv_w, conv_b, w_down):
    S = h.shape[1]
    g = h @ w_gate
    g_pad = jnp.pad(g, ((0, 0), (CONV_WIDTH - 1, 0), (0, 0)))
    g_conv = conv_b
    for tap in range(CONV_WIDTH):
        g_conv = g_conv + conv_w[tap] * g_pad[:, tap:tap + S]
    return (jax.nn.silu(g_conv) * (h @ w_up)) @ w_down


def setup_inputs(seed: int = 0) -> dict:
    key = jax.random.key(seed)
    keys = iter(jax.random.split(key, 40))
    L = DEPTH

    def nrm(shape, scale):
        return jax.random.normal(next(keys), shape, jnp.float32) * scale

    def gain(shape):
        return 1.0 + 0.01 * jax.random.normal(next(keys), shape, jnp.float32)

    x = nrm((BATCH, SEQ, D_MODEL), 1.0)
    c = nrm((BATCH, D_MODEL), 1.0)
    offsets = jax.random.randint(next(keys), (BATCH, 1), 0, 4096, dtype=jnp.int32)
    positions = offsets + jnp.arange(SEQ, dtype=jnp.int32)[None, :]
    return {
        "x": x,
        "c": c,
        "positions": positions,
        "rel_bias_table": nrm((REL_BUCKETS, NSA_HEADS), 0.2),
        "ada_w": nrm((L, D_MODEL, 6 * D_MODEL), D_MODEL ** -0.5),
        "ada_b": nrm((L, 6 * D_MODEL), 0.01),
        "norm_mix_g": gain((L, D_MODEL)),
        "w_in": nrm((L, D_MODEL, IN_COLS), D_MODEL ** -0.5),
        "cmp_pos_k": nrm((L, CMP_LEN, NSA_DK), 0.1),
        "cmp_w1_k": nrm((L, CMP_LEN * NSA_DK, CMP_HIDDEN), (CMP_LEN * NSA_DK) ** -0.5),
        "cmp_w2_k": nrm((L, CMP_HIDDEN, NSA_DK), CMP_HIDDEN ** -0.5),
        "cmp_pos_v": nrm((L, CMP_LEN, NSA_DV), 0.1),
        "cmp_w1_v": nrm((L, CMP_LEN * NSA_DV, CMP_HIDDEN), (CMP_LEN * NSA_DV) ** -0.5),
        "cmp_w2_v": nrm((L, CMP_HIDDEN, NSA_DV), CMP_HIDDEN ** -0.5),
        "mla_q_norm_g": gain((L, MLA_Q_RANK)),
        "mla_w_uq": nrm((L, MLA_Q_RANK, MLA_HEADS * (MLA_NOPE + MLA_ROPE)), MLA_Q_RANK ** -0.5),
        "mla_kv_norm_g": gain((L, MLA_KV_RANK)),
        "mla_w_ukv": nrm((L, MLA_KV_RANK, MLA_HEADS * (MLA_NOPE + MLA_V)), MLA_KV_RANK ** -0.5),
        "w_o": nrm((L, D_MODEL, D_MODEL), D_MODEL ** -0.5),
        "norm_ffn_g": gain((L, D_MODEL)),
        "ffn_w_gate": nrm((L, D_MODEL, D_FF), D_MODEL ** -0.5),
        "ffn_w_up": nrm((L, D_MODEL, D_FF), D_MODEL ** -0.5),
        "ffn_conv_w": nrm((L, CONV_WIDTH, D_FF), CONV_WIDTH ** -0.5),
        "ffn_conv_b": nrm((L, D_FF), 0.01),
        "ffn_w_down": nrm((L, D_FF, D_MODEL), D_FF ** -0.5),
        "final_norm_g": gain((D_MODEL,)),
    }


def reference(x, c, positions, rel_bias_table, ada_w, ada_b, norm_mix_g, w_in,
              cmp_pos_k, cmp_w1_k, cmp_w2_k, cmp_pos_v, cmp_w1_v, cmp_w2_v,
              mla_q_norm_g, mla_w_uq, mla_kv_norm_g, mla_w_ukv, w_o,
              norm_ffn_g, ffn_w_gate, ffn_w_up, ffn_conv_w, ffn_conv_b, ffn_w_down,
              final_norm_g):
    cond = jax.nn.silu(c)
    for layer in range(DEPTH):
        mod = (cond @ ada_w[layer] + ada_b[layer])[:, None, :]
        shift_m, scale_m, gate_m, shift_f, scale_f, gate_f = jnp.split(mod, 6, axis=-1)
        h = rmsnorm(x, norm_mix_g[layer]) * (1.0 + scale_m) + shift_m
        x = x + gate_m * hybrid_mixer(h, positions, rel_bias_table, w_in[layer],
                                      cmp_pos_k[layer], cmp_w1_k[layer], cmp_w2_k[layer],
                                      cmp_pos_v[layer], cmp_w1_v[layer], cmp_w2_v[layer],
                                      mla_q_norm_g[layer], mla_w_uq[layer],
                                      mla_kv_norm_g[layer], mla_w_ukv[layer], w_o[layer])
        h = rmsnorm(x, norm_ffn_g[layer]) * (1.0 + scale_f) + shift_f
        x = x + gate_f * conv_glu_ffn(h, ffn_w_gate[layer], ffn_w_up[layer],
                                      ffn_conv_w[layer], ffn_conv_b[layer], ffn_w_down[layer])
    return rmsnorm(x, final_norm_g)
```

```python
import functools
import math

import numpy as np
import jax
import jax.numpy as jnp
from jax import lax
from jax.experimental import pallas as pl
from jax.experimental.pallas import tpu as pltpu

F32 = jnp.float32
BF16 = jnp.bfloat16
HIGHEST = lax.Precision.HIGHEST

D_MODEL = 1024
NSA_HEADS = 16
NSA_GROUPS = 4
NSA_HPG = 4
NSA_D = 64
CMP_LEN = 32
CMP_STRIDE = 16
CMP_HIDDEN = 256
SEL_BLOCK = 64
SEL_TOPK = 16
WINDOW = 512
MLA_HEADS = 8
MLA_Q_RANK = 256
MLA_KV_RANK = 128
MLA_NOPE = 64
MLA_ROPE = 32
MLA_V = 128
ROPE_THETA = 10000.0
REL_BUCKETS = 32
REL_MAX_DIST = 128
D_FF = 2816
RMS_EPS = 1e-6
NEG = -1e30

LANES = 128
TILE = 128
N_CMP_PAD = 128
N_SEL_BLOCKS = 32

ZC_Q = 0
ZC_MA = 1024
ZC_MB = 2048
ZC_KS = 3072
ZC_VS = 3328
ZC_KW = 3584
ZC_VW = 3840
ZC_CQ = 4096
ZC_CKV = 4352
ZC_MISC = 4480
Z_WIDTH = 4608
CMP_WIDTH = 512
GATE_LANE0 = MLA_ROPE

VMEM_LIMIT = 56 * 1024 * 1024

_NT = (((1,), (1,)), ((), ()))


def _cparams(n_axes):
    return pltpu.CompilerParams(
        dimension_semantics=("arbitrary",) * n_axes,
        vmem_limit_bytes=VMEM_LIMIT)


def _sigmoid(x):
    return 1.0 / (1.0 + jnp.exp(-x))


def _t5_bucket(dist):
    n = jnp.maximum(dist, 0)
    exact = REL_BUCKETS // 2
    nf = jnp.maximum(n, exact).astype(F32)
    log_ratio = jnp.log(nf / exact) / math.log(REL_MAX_DIST / exact)
    large = jnp.minimum(exact + (log_ratio * (REL_BUCKETS - exact)).astype(jnp.int32),
                        REL_BUCKETS - 1)
    return jnp.where(n < exact, n, large)


def _bias_kernel(rel_ref, out_ref, *, col_stride, offset, limit, sub_far):
    m = pl.program_id(0)
    r = lax.broadcasted_iota(jnp.int32, (TILE, LANES), 0)
    c = lax.broadcasted_iota(jnp.int32, (TILE, LANES), 1)
    dist = m * TILE + r - c * col_stride + offset
    valid = dist >= 0
    if limit is not None:
        valid = valid & (dist < limit)
    bucket = _t5_bucket(dist)
    for h in range(NSA_HEADS):
        val = jnp.zeros((TILE, LANES), F32)
        for b in range(REL_BUCKETS):
            val = jnp.where(bucket == b, rel_ref[b, h], val)
        if sub_far:
            val = val - rel_ref[REL_BUCKETS - 1, h]
        out_ref[0, h] = jnp.where(valid, val, NEG)


def _bias_table(rel, n_tiles, *, col_stride, offset, limit, sub_far):
    return pl.pallas_call(
        functools.partial(_bias_kernel, col_stride=col_stride, offset=offset,
                          limit=limit, sub_far=sub_far),
        grid=(n_tiles,),
        in_specs=[pl.BlockSpec(memory_space=pltpu.SMEM)],
        out_specs=pl.BlockSpec((1, NSA_HEADS, TILE, LANES), lambda m: (m, 0, 0, 0)),
        out_shape=jax.ShapeDtypeStruct((n_tiles, NSA_HEADS, TILE, LANES), F32),
        compiler_params=_cparams(1),
        name="bias_table",
    )(rel)


def _ada_kernel(c_ref, w_ref, b_ref, o_ref):
    c = c_ref[...]
    cond = c * _sigmoid(c)
    o_ref[...] = jnp.dot(cond, w_ref[...], precision=HIGHEST,
                         preferred_element_type=F32) + b_ref[...]


def _ada(c, w, b):
    bsz = c.shape[0]
    n = w.shape[1]
    tn = 1024
    return pl.pallas_call(
        _ada_kernel,
        grid=(n // tn,),
        in_specs=[pl.BlockSpec((bsz, D_MODEL), lambda j: (0, 0)),
                  pl.BlockSpec((D_MODEL, tn), lambda j: (0, j)),
                  pl.BlockSpec((1, tn), lambda j: (0, j))],
        out_specs=pl.BlockSpec((bsz, tn), lambda j: (0, j)),
        out_shape=jax.ShapeDtypeStruct((bsz, n), F32),
        compiler_params=_cparams(1),
        name="ada_mod",
    )(c, w, b.reshape(1, n))


def _modulated_norm(x, gain, shift, scale):
    ms = jnp.mean(x * x, axis=-1, keepdims=True)
    return (x * lax.rsqrt(ms + RMS_EPS) * gain) * (1.0 + scale) + shift


def _inproj_kernel(x_ref, mod_ref, g_ref, w_ref, z_ref, cmp_ref, *, tn):
    mod = mod_ref[0]
    h = _modulated_norm(x_ref[0], g_ref[...], mod[:, 0:D_MODEL],
                        mod[:, D_MODEL:2 * D_MODEL])
    hb = h.astype(BF16)
    for n0 in range(0, Z_WIDTH, tn):
        z_ref[0, :, n0:n0 + tn] = jnp.dot(
            hb, w_ref[:, n0:n0 + tn], preferred_element_type=F32).astype(BF16)
    acc = jnp.dot(hb, w_ref[:, Z_WIDTH:Z_WIDTH + CMP_WIDTH], preferred_element_type=F32)
    for j in range(2 * NSA_GROUPS):
        cmp_ref[0, j] = acc[:, j * NSA_D:(j + 1) * NSA_D].astype(BF16)


def _inproj(x, mod3, gain, w_p, tm=512):
    bsz, seq, _ = x.shape
    ncol = w_p.shape[1]
    return pl.pallas_call(
        functools.partial(_inproj_kernel, tn=512),
        grid=(bsz, seq // tm),
        in_specs=[pl.BlockSpec((1, tm, D_MODEL), lambda b, i: (b, i, 0)),
                  pl.BlockSpec((1, 1, 6 * D_MODEL), lambda b, i: (b, 0, 0)),
                  pl.BlockSpec((1, D_MODEL), lambda b, i: (0, 0)),
                  pl.BlockSpec((D_MODEL, ncol), lambda b, i: (0, 0))],
        out_specs=[pl.BlockSpec((1, tm, Z_WIDTH), lambda b, i: (b, i, 0)),
                   pl.BlockSpec((1, 2 * NSA_GROUPS, tm, NSA_D), lambda b, i: (b, 0, i, 0))],
        out_shape=[jax.ShapeDtypeStruct((bsz, seq, Z_WIDTH), BF16),
                   jax.ShapeDtypeStruct((bsz, 2 * NSA_GROUPS, seq, NSA_D), BF16)],
        compiler_params=_cparams(2),
        name="in_proj",
    )(x, mod3, gain, w_p)


def _gelu_tanh(x):
    return 0.5 * x * (1.0 + jnp.tanh(math.sqrt(2.0 / math.pi) * (x + 0.044715 * (x * x * x))))


def _compress_kernel(y_ref, pos_ref, w1_ref, w2_ref, o_ref):
    y = y_ref[0, 0]
    w1 = w1_ref[0]
    half = (CMP_LEN // 2) * NSA_D
    top = jnp.dot(y, w1[:half].astype(BF16), preferred_element_type=F32)
    bot = jnp.dot(y, w1[half:].astype(BF16), preferred_element_type=F32)
    pos = jnp.broadcast_to(pos_ref[0], (8, CMP_LEN * NSA_D))
    pterm = jnp.dot(pos, w1, precision=HIGHEST, preferred_element_type=F32)[0:1]
    hidden = top + pltpu.roll(bot, N_CMP_PAD - 1, 0) + pterm
    act = _gelu_tanh(hidden).astype(BF16)
    o_ref[0, 0] = jnp.dot(act, w2_ref[0].astype(BF16),
                          preferred_element_type=F32).astype(BF16)


def _compress(cmp_in, pos, w1, w2):
    bsz, nkg, seq, _ = cmp_in.shape
    nhb = seq // CMP_STRIDE
    y = cmp_in.reshape(bsz, nkg, nhb, CMP_STRIDE * NSA_D)
    return pl.pallas_call(
        _compress_kernel,
        grid=(bsz, nkg),
        in_specs=[pl.BlockSpec((1, 1, nhb, CMP_STRIDE * NSA_D), lambda b, j: (b, j, 0, 0)),
                  pl.BlockSpec((1, 1, CMP_LEN * NSA_D), lambda b, j: (j // NSA_GROUPS, 0, 0)),
                  pl.BlockSpec((1, CMP_LEN * NSA_D, CMP_HIDDEN), lambda b, j: (j // NSA_GROUPS, 0, 0)),
                  pl.BlockSpec((1, CMP_HIDDEN, NSA_D), lambda b, j: (j // NSA_GROUPS, 0, 0))],
        out_specs=pl.BlockSpec((1, 1, nhb, NSA_D), lambda b, j: (b, j, 0, 0)),
        out_shape=jax.ShapeDtypeStruct((bsz, nkg, nhb, NSA_D), BF16),
        compiler_params=_cparams(2),
        name="compress",
    )(y, pos, w1, w2)


def _cmp_attn_kernel(q_ref, misc_ref, kc_ref, vc_ref, bias_ref, ovl_ref, o_ref, mb_ref, *, tq):
    i = pl.program_id(1)
    gates = _sigmoid(misc_ref[0].astype(F32))
    eye = (lax.broadcasted_iota(jnp.int32, (tq, tq), 0)
           == lax.broadcasted_iota(jnp.int32, (tq, tq), 1)).astype(BF16)
    jj = lax.broadcasted_iota(jnp.int32, (N_SEL_BLOCKS, tq), 0)
    tt = i * tq + lax.broadcasted_iota(jnp.int32, (N_SEL_BLOCKS, tq), 1)
    cur = tt // SEL_BLOCK
    forced = (jj == 0) | (jj == cur) | (jj == cur - 1)
    for g in range(NSA_GROUPS):
        kc = kc_ref[0, g]
        vc = vc_ref[0, g]
        psum = jnp.zeros((tq, N_CMP_PAD), F32)
        for hh in range(NSA_HPG):
            h = g * NSA_HPG + hh
            qh = q_ref[0, :, h * NSA_D:(h + 1) * NSA_D]
            b = bias_ref[h]
            s = lax.dot_general(qh, kc, _NT, preferred_element_type=F32) + b
            m = jnp.max(s, axis=-1, keepdims=True)
            e = jnp.exp(s - m)
            p = jnp.where(b > 0.5 * NEG, e / jnp.sum(e, axis=-1, keepdims=True), 0.0)
            psum = psum + p
            o = jnp.dot(p.astype(BF16), vc, preferred_element_type=F32)
            lane = GATE_LANE0 + 3 * h
            o_ref[0, :, h * NSA_D:(h + 1) * NSA_D] = (o * gates[:, lane:lane + 1]).astype(BF16)
        score = lax.dot_general(ovl_ref[...], psum, _NT, precision=HIGHEST,
                                preferred_element_type=F32)
        score = jnp.where(forced, jnp.inf, jnp.where(jj > cur, -jnp.inf, score))
        rank = jnp.zeros((N_SEL_BLOCKS, tq), jnp.int32)
        for j2 in range(N_SEL_BLOCKS):
            row = score[j2:j2 + 1, :]
            beats = (row > score) | ((row == score) & (jj > j2))
            rank = rank + beats.astype(jnp.int32)
        sel_t = jnp.where(rank < SEL_TOPK, 1.0, 0.0).astype(BF16)
        sel = lax.dot_general(eye, sel_t, _NT, preferred_element_type=F32)
        mb_ref[0, :, g * N_SEL_BLOCKS:(g + 1) * N_SEL_BLOCKS] = jnp.where(
            sel > 0.5, 0.0, NEG).astype(BF16)


def _cmp_attn(z, kvc, bias_c, ovl_t, tq=256):
    bsz, seq, _ = z.shape
    return pl.pallas_call(
        functools.partial(_cmp_attn_kernel, tq=tq),
        grid=(bsz, seq // tq),
        in_specs=[pl.BlockSpec((1, tq, D_MODEL), lambda b, i: (b, i, ZC_Q // D_MODEL)),
                  pl.BlockSpec((1, tq, LANES), lambda b, i: (b, i, ZC_MISC // LANES)),
                  pl.BlockSpec((1, NSA_GROUPS, N_CMP_PAD, NSA_D), lambda b, i: (b, 0, 0, 0)),
                  pl.BlockSpec((1, NSA_GROUPS, N_CMP_PAD, NSA_D), lambda b, i: (b, 1, 0, 0)),
                  pl.BlockSpec((NSA_HEADS, tq, N_CMP_PAD), lambda b, i: (0, i, 0)),
                  pl.BlockSpec((N_SEL_BLOCKS, N_CMP_PAD), lambda b, i: (0, 0))],
        out_specs=[pl.BlockSpec((1, tq, D_MODEL), lambda b, i: (b, i, 0)),
                   pl.BlockSpec((1, tq, LANES), lambda b, i: (b, i, 0))],
        out_shape=[jax.ShapeDtypeStruct((bsz, seq, D_MODEL), BF16),
                   jax.ShapeDtypeStruct((bsz, seq, LANES), BF16)],
        compiler_params=_cparams(2),
        name="cmp_attn_select",
    )(z, z, kvc, kvc, bias_c, ovl_t)


def _flash_update(s, v, m, l, acc):
    m_new = jnp.maximum(m, jnp.max(s, axis=-1, keepdims=True))
    alpha = jnp.exp(m - m_new)
    p = jnp.exp(s - m_new)
    l_new = alpha * l + jnp.sum(p, axis=-1, keepdims=True)
    acc_new = alpha * acc + jnp.dot(p.astype(BF16), v, preferred_element_type=F32)
    return m_new, l_new, acc_new


def _store_heads(o_ref, out, gates, g, gate_idx):
    for hh in range(NSA_HPG):
        h = g * NSA_HPG + hh
        lane = GATE_LANE0 + 3 * h + gate_idx
        o = out[hh * TILE:(hh + 1) * TILE, 0:NSA_D] * gates[:, lane:lane + 1]
        o_ref[0, :, h * NSA_D:(h + 1) * NSA_D] = o.astype(BF16)


def _slc_kernel(q_ref, mb_ref, misc_ref, k_ref, v_ref, bias_ref, o_ref, kaug, vaug):
    i = pl.program_id(1)
    seq = k_ref.shape[1]

    @pl.when(i == 0)
    def _():
        blk = lax.broadcasted_iota(jnp.int32, (seq, NSA_D), 0) // SEL_BLOCK
        lane = lax.broadcasted_iota(jnp.int32, (seq, NSA_D), 1)
        onehot = jnp.where(blk == lane, 1.0, 0.0).astype(BF16)
        zeros = jnp.zeros((seq, NSA_D), BF16)
        for g in range(NSA_GROUPS):
            kaug[g] = jnp.concatenate([k_ref[0, :, g * NSA_D:(g + 1) * NSA_D], onehot], axis=1)
            vaug[g] = jnp.concatenate([v_ref[0, :, g * NSA_D:(g + 1) * NSA_D], zeros], axis=1)

    gates = _sigmoid(misc_ref[0].astype(F32))
    pad = jnp.zeros((TILE, LANES - NSA_D - N_SEL_BLOCKS), BF16)
    for g in range(NSA_GROUPS):
        mbg = mb_ref[0, :, g * N_SEL_BLOCKS:(g + 1) * N_SEL_BLOCKS]
        qs = jnp.concatenate(
            [jnp.concatenate([q_ref[0, :, h * NSA_D:(h + 1) * NSA_D], mbg, pad], axis=1)
             for h in range(g * NSA_HPG, (g + 1) * NSA_HPG)], axis=0)

        def body(kt, carry, g=g, qs=qs):
            m, l, acc = carry
            off = pl.multiple_of(kt * TILE, TILE)
            s = lax.dot_general(qs, kaug[g, pl.ds(off, TILE), :], _NT,
                                preferred_element_type=F32)
            d = jnp.minimum(i - kt, 2)
            s = s + bias_ref[d, pl.ds(g * NSA_HPG, NSA_HPG)].reshape(NSA_HPG * TILE, LANES)
            return _flash_update(s, vaug[g, pl.ds(off, TILE), :], m, l, acc)

        init = (jnp.full((NSA_HPG * TILE, 1), NEG, F32),
                jnp.zeros((NSA_HPG * TILE, 1), F32),
                jnp.zeros((NSA_HPG * TILE, LANES), F32))
        m, l, acc = lax.fori_loop(0, i + 1, body, init)
        _store_heads(o_ref, acc / l, gates, g, 1)


def _slc_attn(z, mb, bias_s):
    bsz, seq, _ = z.shape
    return pl.pallas_call(
        _slc_kernel,
        grid=(bsz, seq // TILE),
        in_specs=[pl.BlockSpec((1, TILE, D_MODEL), lambda b, i: (b, i, ZC_Q // D_MODEL)),
                  pl.BlockSpec((1, TILE, LANES), lambda b, i: (b, i, 0)),
                  pl.BlockSpec((1, TILE, LANES), lambda b, i: (b, i, ZC_MISC // LANES)),
                  pl.BlockSpec((1, seq, 256), lambda b, i: (b, 0, ZC_KS // 256)),
                  pl.BlockSpec((1, seq, 256), lambda b, i: (b, 0, ZC_VS // 256)),
                  pl.BlockSpec((3, NSA_HEADS, TILE, LANES), lambda b, i: (0, 0, 0, 0))],
        out_specs=pl.BlockSpec((1, TILE, D_MODEL), lambda b, i: (b, i, 0)),
        out_shape=jax.ShapeDtypeStruct((bsz, seq, D_MODEL), BF16),
        scratch_shapes=[pltpu.VMEM((NSA_GROUPS, seq, LANES), BF16),
                        pltpu.VMEM((NSA_GROUPS, seq, LANES), BF16)],
        compiler_params=_cparams(2),
        name="slc_attn",
    )(z, mb, z, z, z, bias_s)


def _win_kernel(q_ref, misc_ref, k_ref, v_ref, bias_ref, o_ref, kpad, vpad):
    i = pl.program_id(1)
    seq = k_ref.shape[1]
    n_tiles = WINDOW // TILE + 1

    @pl.when(i == 0)
    def _():
        zeros = jnp.zeros((seq, NSA_D), BF16)
        for g in range(NSA_GROUPS):
            kpad[g] = jnp.concatenate([k_ref[0, :, g * NSA_D:(g + 1) * NSA_D], zeros], axis=1)
            vpad[g] = jnp.concatenate([v_ref[0, :, g * NSA_D:(g + 1) * NSA_D], zeros], axis=1)

    gates = _sigmoid(misc_ref[0].astype(F32))
    zq = jnp.zeros((TILE, LANES - NSA_D), BF16)
    for g in range(NSA_GROUPS):
        qs = jnp.concatenate(
            [jnp.concatenate([q_ref[0, :, h * NSA_D:(h + 1) * NSA_D], zq], axis=1)
             for h in range(g * NSA_HPG, (g + 1) * NSA_HPG)], axis=0)
        logits, offs = [], []
        for dlt in range(n_tiles):
            kt = i - dlt
            off = pl.multiple_of(jnp.maximum(kt, 0) * TILE, TILE)
            s = lax.dot_general(qs, kpad[g, pl.ds(off, TILE), :], _NT,
                                preferred_element_type=F32)
            s = s + bias_ref[dlt, g * NSA_HPG:(g + 1) * NSA_HPG].reshape(NSA_HPG * TILE, LANES)
            logits.append(jnp.where(kt >= 0, s, NEG))
            offs.append(off)
        m = functools.reduce(jnp.maximum,
                             [jnp.max(s, axis=-1, keepdims=True) for s in logits])
        l = jnp.zeros((NSA_HPG * TILE, 1), F32)
        acc = jnp.zeros((NSA_HPG * TILE, LANES), F32)
        for s, off in zip(logits, offs):
            p = jnp.exp(s - m)
            l = l + jnp.sum(p, axis=-1, keepdims=True)
            acc = acc + jnp.dot(p.astype(BF16), vpad[g, pl.ds(off, TILE), :],
                                preferred_element_type=F32)
        _store_heads(o_ref, acc / l, gates, g, 2)


def _win_attn(z, bias_w):
    bsz, seq, _ = z.shape
    n_tiles = WINDOW // TILE + 1
    return pl.pallas_call(
        _win_kernel,
        grid=(bsz, seq // TILE),
        in_specs=[pl.BlockSpec((1, TILE, D_MODEL), lambda b, i: (b, i, ZC_Q // D_MODEL)),
                  pl.BlockSpec((1, TILE, LANES), lambda b, i: (b, i, ZC_MISC // LANES)),
                  pl.BlockSpec((1, seq, 256), lambda b, i: (b, 0, ZC_KW // 256)),
                  pl.BlockSpec((1, seq, 256), lambda b, i: (b, 0, ZC_VW // 256)),
                  pl.BlockSpec((n_tiles, NSA_HEADS, TILE, LANES), lambda b, i: (0, 0, 0, 0))],
        out_specs=pl.BlockSpec((1, TILE, D_MODEL), lambda b, i: (b, i, 0)),
        out_shape=jax.ShapeDtypeStruct((bsz, seq, D_MODEL), BF16),
        scratch_shapes=[pltpu.VMEM((NSA_GROUPS, seq, LANES), BF16),
                        pltpu.VMEM((NSA_GROUPS, seq, LANES), BF16)],
        compiler_params=_cparams(2),
        name="win_attn",
    )(z, z, z, z, bias_w)


def _rms(x, gain):
    ms = jnp.mean(x * x, axis=-1, keepdims=True)
    return x * lax.rsqrt(ms + RMS_EPS) * gain


def _mla_prep_kernel(cq_ref, ckv_ref, misc_ref, pos_ref, qg_ref, kvg_ref, wqa_ref, wqb_ref,
                     wk_ref, wv_ref, pa_ref, pb_ref, frq_ref, q_out, k_out, v_out):
    cqn = _rms(cq_ref[0].astype(F32), qg_ref[...]).astype(BF16)
    ckvn = _rms(ckv_ref[0].astype(F32), kvg_ref[...]).astype(BF16)
    ang = pos_ref[0].astype(F32) * frq_ref[...]
    cos = jnp.cos(ang)
    sin = jnp.sin(ang)
    scale = (MLA_NOPE + MLA_ROPE) ** -0.5
    qa = jnp.dot(cqn, wqa_ref[...], preferred_element_type=F32)
    qb = jnp.dot(cqn, wqb_ref[...], preferred_element_type=F32)
    misc = misc_ref[0]
    kr = (jnp.dot(misc, pa_ref[...], preferred_element_type=F32) * cos
          + jnp.dot(misc, pb_ref[...], preferred_element_type=F32) * sin)
    kn = jnp.dot(ckvn, wk_ref[...], preferred_element_type=F32)
    v_out[0] = jnp.dot(ckvn, wv_ref[...], preferred_element_type=F32).astype(BF16)
    for h in range(MLA_HEADS):
        sl = slice(h * LANES, (h + 1) * LANES)
        q_out[0, :, sl] = ((qa[:, sl] * cos + qb[:, sl] * sin) * scale).astype(BF16)
        k_out[0, :, sl] = (kn[:, sl] + kr).astype(BF16)


def _mla_prep(z, pos3, qg, kvg, wqa, wqb, wk, wv, pa, pb, frq, tm=512):
    bsz, seq, _ = z.shape
    full = lambda shape: pl.BlockSpec(shape, lambda b, i: (0,) * len(shape))
    out = jax.ShapeDtypeStruct((bsz, seq, MLA_HEADS * LANES), BF16)
    ospec = pl.BlockSpec((1, tm, MLA_HEADS * LANES), lambda b, i: (b, i, 0))
    return pl.pallas_call(
        _mla_prep_kernel,
        grid=(bsz, seq // tm),
        in_specs=[pl.BlockSpec((1, tm, MLA_Q_RANK), lambda b, i: (b, i, ZC_CQ // MLA_Q_RANK)),
                  pl.BlockSpec((1, tm, MLA_KV_RANK), lambda b, i: (b, i, ZC_CKV // MLA_KV_RANK)),
                  pl.BlockSpec((1, tm, LANES), lambda b, i: (b, i, ZC_MISC // LANES)),
                  pl.BlockSpec((1, tm, 1), lambda b, i: (b, i, 0)),
                  full((1, MLA_Q_RANK)), full((1, MLA_KV_RANK)),
                  full((MLA_Q_RANK, MLA_HEADS * LANES)), full((MLA_Q_RANK, MLA_HEADS * LANES)),
                  full((MLA_KV_RANK, MLA_HEADS * LANES)), full((MLA_KV_RANK, MLA_HEADS * LANES)),
                  full((LANES, LANES)), full((LANES, LANES)), full((1, LANES))],
        out_specs=[ospec, ospec, ospec],
        out_shape=[out, out, out],
        compiler_params=_cparams(2),
        name="mla_prep",
    )(z, z, z, pos3, qg, kvg, wqa, wqb, wk, wv, pa, pb, frq)


def _mla_attn_kernel(q_ref, k_ref, v_ref, o_ref, *, tq):
    i = pl.program_id(1)
    row = lax.broadcasted_iota(jnp.int32, (tq, tq), 0)
    col = lax.broadcasted_iota(jnp.int32, (tq, tq), 1)
    for h in range(MLA_HEADS):
        sl = slice(h * LANES, (h + 1) * LANES)
        q = q_ref[0, :, sl]

        def body(kt, carry, q=q, sl=sl):
            off = pl.multiple_of(kt * tq, tq)
            s = lax.dot_general(q, k_ref[0, pl.ds(off, tq), sl], _NT,
                                preferred_element_type=F32)
            return _flash_update(s, v_ref[0, pl.ds(off, tq), sl], *carry)

        init = (jnp.full((tq, 1), NEG, F32), jnp.zeros((tq, 1), F32),
                jnp.zeros((tq, LANES), F32))
        carry = lax.fori_loop(0, i, body, init)
        off = pl.multiple_of(i * tq, tq)
        s = lax.dot_general(q, k_ref[0, pl.ds(off, tq), sl], _NT, preferred_element_type=F32)
        s = jnp.where(row >= col, s, NEG)
        m, l, acc = _flash_update(s, v_ref[0, pl.ds(off, tq), sl], *carry)
        o_ref[0, :, sl] = (acc / l).astype(BF16)


def _mla_attn(q, k, v, tq=256):
    bsz, seq, width = q.shape
    return pl.pallas_call(
        functools.partial(_mla_attn_kernel, tq=tq),
        grid=(bsz, seq // tq),
        in_specs=[pl.BlockSpec((1, tq, width), lambda b, i: (b, i, 0)),
                  pl.BlockSpec((1, seq, width), lambda b, i: (b, 0, 0)),
                  pl.BlockSpec((1, seq, width), lambda b, i: (b, 0, 0))],
        out_specs=pl.BlockSpec((1, tq, width), lambda b, i: (b, i, 0)),
        out_shape=jax.ShapeDtypeStruct((bsz, seq, width), BF16),
        compiler_params=_cparams(2),
        name="mla_attn",
    )(q, k, v)


def _merge_kernel(x_ref, mod_ref, oc_ref, os_ref, ow_ref, om_ref, ma_ref, mb_ref, wo_ref,
                  g_ref, x1_ref, h2_ref):
    mod = mod_ref[0]
    o_nsa = oc_ref[0].astype(F32) + os_ref[0].astype(F32) + ow_ref[0].astype(F32)
    y = (_sigmoid(ma_ref[0].astype(F32)) * o_nsa
         + _sigmoid(mb_ref[0].astype(F32)) * om_ref[0].astype(F32))
    a = jnp.dot(y.astype(BF16), wo_ref[...], preferred_element_type=F32)
    x1 = x_ref[0] + mod[:, 2 * D_MODEL:3 * D_MODEL] * a
    x1_ref[0] = x1
    h2 = _modulated_norm(x1, g_ref[...], mod[:, 3 * D_MODEL:4 * D_MODEL],
                         mod[:, 4 * D_MODEL:5 * D_MODEL])
    h2_ref[0] = h2.astype(BF16)


def _merge(x, mod3, o_cmp, o_slc, o_win, o_mla, z, w_o, gain, tm=512):
    bsz, seq, _ = x.shape
    tok = lambda col: pl.BlockSpec((1, tm, D_MODEL), lambda b, i: (b, i, col))
    return pl.pallas_call(
        _merge_kernel,
        grid=(bsz, seq // tm),
        in_specs=[tok(0),
                  pl.BlockSpec((1, 1, 6 * D_MODEL), lambda b, i: (b, 0, 0)),
                  tok(0), tok(0), tok(0), tok(0),
                  tok(ZC_MA // D_MODEL), tok(ZC_MB // D_MODEL),
                  pl.BlockSpec((D_MODEL, D_MODEL), lambda b, i: (0, 0)),
                  pl.BlockSpec((1, D_MODEL), lambda b, i: (0, 0))],
        out_specs=[tok(0), tok(0)],
        out_shape=[jax.ShapeDtypeStruct((bsz, seq, D_MODEL), F32),
                   jax.ShapeDtypeStruct((bsz, seq, D_MODEL), BF16)],
        compiler_params=_cparams(2),
        name="merge_out_proj",
    )(x, mod3, o_cmp, o_slc, o_win, o_mla, z, z, w_o, gain)


def _ffn_kernel(x1_ref, h2_ref, mod_ref, wg_ref, wu_ref, cw_ref, cb_ref, wd_ref, fg_ref,
                o_ref, prev_ref, *, tm, tf):
    i = pl.program_id(1)

    @pl.when(i == 0)
    def _():
        prev_ref[...] = jnp.zeros_like(prev_ref)

    h2 = h2_ref[0]
    row = lax.broadcasted_iota(jnp.int32, (tm, tf), 0)
    acc = jnp.zeros((tm, D_MODEL), F32)
    for f0 in range(0, D_FF, tf):
        fs = slice(f0, f0 + tf)
        gt = jnp.dot(h2, wg_ref[:, fs], preferred_element_type=F32)
        p1 = prev_ref[7:8, fs]
        p2 = prev_ref[6:7, fs]
        g1 = jnp.where(row == 0, p1, pltpu.roll(gt, 1, 0))
        g2 = jnp.where(row == 0, p2, jnp.where(row == 1, p1, pltpu.roll(gt, 2, 0)))
        prev_ref[:, fs] = gt[tm - 8:tm, :]
        conv = cb_ref[:, fs] + cw_ref[0:1, fs] * g2 + cw_ref[1:2, fs] * g1 + cw_ref[2:3, fs] * gt
        up = jnp.dot(h2, wu_ref[:, fs], preferred_element_type=F32)
        act = (conv * _sigmoid(conv)) * up
        acc = acc + jnp.dot(act.astype(BF16), wd_ref[fs, :], preferred_element_type=F32)
    mod = mod_ref[0]
    x2 = x1_ref[0] + mod[:, 5 * D_MODEL:6 * D_MODEL] * acc
    ms = jnp.mean(x2 * x2, axis=-1, keepdims=True)
    o_ref[0] = x2 * lax.rsqrt(ms + RMS_EPS) * fg_ref[...]


def _ffn(x1, h2, mod3, wg, wu, cw, cb, wd, fg, tm=512, tf=1408):
    bsz, seq, _ = x1.shape
    tok = pl.BlockSpec((1, tm, D_MODEL), lambda b, i: (b, i, 0))
    const = lambda shape: pl.BlockSpec(shape, lambda b, i: (0,) * len(shape),
                                       pipeline_mode=pl.Buffered(1))
    return pl.pallas_call(
        functools.partial(_ffn_kernel, tm=tm, tf=tf),
        grid=(bsz, seq // tm),
        in_specs=[tok, tok,
                  pl.BlockSpec((1, 1, 6 * D_MODEL), lambda b, i: (b, 0, 0)),
                  const((D_MODEL, D_FF)), const((D_MODEL, D_FF)),
                  const((3, D_FF)), const((1, D_FF)), const((D_FF, D_MODEL)),
                  const((1, D_MODEL))],
        out_specs=tok,
        out_shape=jax.ShapeDtypeStruct((bsz, seq, D_MODEL), F32),
        scratch_shapes=[pltpu.VMEM((8, D_FF), F32)],
        compiler_params=_cparams(2),
        name="conv_glu_ffn",
    )(x1, h2, mod3, wg, wu, cw, cb, wd, fg)


def _prep_w_in(w_in):
    off = {}
    o = 0
    for name, width in (("nsa_q", 1024), ("k_cmp", 256), ("v_cmp", 256), ("k_slc", 256),
                        ("v_slc", 256), ("k_win", 256), ("v_win", 256), ("nsa_gate", 48),
                        ("mla_cq", 256), ("mla_ckv", 128), ("mla_krope", 32),
                        ("merge_a", 1024), ("merge_b", 1024)):
        off[name] = (o, o + width)
        o += width
    col = lambda name: w_in[:, off[name][0]:off[name][1]]
    pad = jnp.zeros((D_MODEL, LANES - MLA_ROPE - 3 * NSA_HEADS), w_in.dtype)
    parts = [col("nsa_q") * (NSA_D ** -0.5),
             col("merge_a"), col("merge_b"), col("k_slc"), col("v_slc"), col("k_win"),
             col("v_win"), col("mla_cq"), col("mla_ckv"), col("mla_krope"), col("nsa_gate"),
             pad, col("k_cmp"), col("v_cmp")]
    return jnp.concatenate(parts, axis=1).astype(BF16)


def _prep_mla_weights(w_uq, w_ukv):
    half = MLA_ROPE // 2
    dq = MLA_NOPE + MLA_ROPE
    wq = w_uq.reshape(MLA_Q_RANK, MLA_HEADS, dq)
    nope, x1, x2 = wq[..., :MLA_NOPE], wq[..., MLA_NOPE:MLA_NOPE + half], wq[..., MLA_NOPE + half:]
    z32 = jnp.zeros((MLA_Q_RANK, MLA_HEADS, LANES - dq), w_uq.dtype)
    wqa = jnp.concatenate([nope, x1, x2, z32], axis=-1)
    wqb = jnp.concatenate([jnp.zeros_like(nope), -x2, x1, z32], axis=-1)
    wkv = w_ukv.reshape(MLA_KV_RANK, MLA_HEADS, MLA_NOPE + MLA_V)
    wk = jnp.concatenate([wkv[..., :MLA_NOPE],
                          jnp.zeros((MLA_KV_RANK, MLA_HEADS, LANES - MLA_NOPE), w_ukv.dtype)],
                         axis=-1)
    wv = wkv[..., MLA_NOPE:]
    flat = lambda w: w.reshape(w.shape[0], MLA_HEADS * LANES).astype(BF16)
    return flat(wqa), flat(wqb), flat(wk), flat(wv)


def _rope_constants():
    half = MLA_ROPE // 2
    pa = np.zeros((LANES, LANES), np.float32)
    pb = np.zeros((LANES, LANES), np.float32)
    for j in range(half):
        pa[j, MLA_NOPE + j] = 1.0
        pa[half + j, MLA_NOPE + half + j] = 1.0
        pb[half + j, MLA_NOPE + j] = -1.0
        pb[j, MLA_NOPE + half + j] = 1.0
    inv_freq = ROPE_THETA ** (-jnp.arange(0, MLA_ROPE, 2, dtype=F32) / MLA_ROPE)
    frq = jnp.zeros((1, LANES), F32)
    frq = frq.at[0, MLA_NOPE:MLA_NOPE + half].set(inv_freq)
    frq = frq.at[0, MLA_NOPE + half:MLA_NOPE + 2 * half].set(inv_freq)
    return jnp.asarray(pa, BF16), jnp.asarray(pb, BF16), frq


def _overlap_t(seq):
    nc = (seq - CMP_LEN) // CMP_STRIDE + 1
    nb = seq // SEL_BLOCK
    cs = np.arange(nc) * CMP_STRIDE
    bs = np.arange(nb) * SEL_BLOCK
    ov = np.clip(np.minimum(cs[:, None] + CMP_LEN, bs[None, :] + SEL_BLOCK)
                 - np.maximum(cs[:, None], bs[None, :]), 0, None) / CMP_LEN
    out = np.zeros((nb, N_CMP_PAD), np.float32)
    out[:, :nc] = ov.T
    return jnp.asarray(out)


def kernel(x, c, positions, rel_bias_table, ada_w, ada_b, norm_mix_g, w_in, cmp_pos_k, cmp_w1_k, cmp_w2_k, cmp_pos_v, cmp_w1_v, cmp_w2_v, mla_q_norm_g, mla_w_uq, mla_kv_norm_g, mla_w_ukv, w_o, norm_ffn_g, ffn_w_gate, ffn_w_up, ffn_conv_w, ffn_conv_b, ffn_w_down, final_norm_g):
    bsz, seq, _ = x.shape
    assert ada_w.shape[0] == 1 and seq == N_SEL_BLOCKS * SEL_BLOCK
    n_qt = seq // TILE

    bias_c = _bias_table(rel_bias_table, n_qt, col_stride=CMP_STRIDE,
                         offset=-(CMP_LEN - 1), limit=None, sub_far=False)
    bias_c = bias_c.transpose(1, 0, 2, 3).reshape(NSA_HEADS, seq, N_CMP_PAD)
    bias_s = _bias_table(rel_bias_table, 3, col_stride=1, offset=0, limit=None, sub_far=True)
    bias_w = _bias_table(rel_bias_table, WINDOW // TILE + 1, col_stride=1, offset=0,
                         limit=WINDOW, sub_far=False)

    mod3 = _ada(c, ada_w[0], ada_b[0]).reshape(bsz, 1, 6 * D_MODEL)
    z, cmp_in = _inproj(x, mod3, norm_mix_g, _prep_w_in(w_in[0]))

    kvc = _compress(cmp_in,
                    jnp.stack([cmp_pos_k[0].reshape(1, -1), cmp_pos_v[0].reshape(1, -1)]),
                    jnp.stack([cmp_w1_k[0], cmp_w1_v[0]]),
                    jnp.stack([cmp_w2_k[0], cmp_w2_v[0]]))
    kvc = kvc.reshape(bsz, 2, NSA_GROUPS, N_CMP_PAD, NSA_D).reshape(
        bsz, 2 * NSA_GROUPS, N_CMP_PAD, NSA_D)

    o_cmp, mb = _cmp_attn(z, kvc, bias_c, _overlap_t(seq))
    o_slc = _slc_attn(z, mb, bias_s)
    o_win = _win_attn(z, bias_w)

    wqa, wqb, wk, wv = _prep_mla_weights(mla_w_uq[0], mla_w_ukv[0])
    pa, pb, frq = _rope_constants()
    q_m, k_m, v_m = _mla_prep(z, positions.reshape(bsz, seq, 1), mla_q_norm_g, mla_kv_norm_g,
                              wqa, wqb, wk, wv, pa, pb, frq)
    o_mla = _mla_attn(q_m, k_m, v_m)

    x1, h2 = _merge(x, mod3, o_cmp, o_slc, o_win, o_mla, z, w_o[0].astype(BF16), norm_ffn_g)
    return _ffn(x1, h2, mod3, ffn_w_gate[0].astype(BF16), ffn_w_up[0].astype(BF16),
                ffn_conv_w[0], ffn_conv_b, ffn_w_down[0].astype(BF16),
                final_norm_g.reshape(1, D_MODEL))
```

```python
import functools
import math

import numpy as np
import jax
import jax.numpy as jnp
from jax import lax
from jax.experimental import pallas as pl
from jax.experimental.pallas import tpu as pltpu

F32 = jnp.float32
BF16 = jnp.bfloat16
HIGHEST = lax.Precision.HIGHEST

D_MODEL = 1024
NSA_HEADS = 16
NSA_GROUPS = 4
NSA_HPG = 4
NSA_D = 64
CMP_LEN = 32
CMP_STRIDE = 16
CMP_HIDDEN = 256
SEL_BLOCK = 64
SEL_TOPK = 16
WINDOW = 512
MLA_HEADS = 8
MLA_Q_RANK = 256
MLA_KV_RANK = 128
MLA_NOPE = 64
MLA_ROPE = 32
MLA_V = 128
ROPE_THETA = 10000.0
REL_BUCKETS = 32
REL_MAX_DIST = 128
D_FF = 2816
RMS_EPS = 1e-6
NEG = -1e30

LANES = 128
TILE = 128
CHUNK = 256
N_CMP_PAD = 128
N_SEL_BLOCKS = 32
MLA_HEADS_PER_TRIP = 4

ZC_Q = 0
ZC_MA = 1024
ZC_MB = 2048
ZC_KS = 3072
ZC_VS = 3328
ZC_KW = 3584
ZC_VW = 3840
ZC_CQ = 4096
ZC_CKV = 4352
ZC_MISC = 4480
Z_WIDTH = 4608
CMP_WIDTH = 512
GATE_LANE0 = MLA_ROPE

VMEM_LIMIT = 56 * 1024 * 1024

_NT = (((1,), (1,)), ((), ()))


def _cparams(n_axes):
    return pltpu.CompilerParams(
        dimension_semantics=("arbitrary",) * n_axes,
        vmem_limit_bytes=VMEM_LIMIT)


def _sigmoid(x):
    return 1.0 / (1.0 + jnp.exp(-x))


def _t5_bucket(dist):
    n = jnp.maximum(dist, 0)
    exact = REL_BUCKETS // 2
    nf = jnp.maximum(n, exact).astype(F32)
    log_ratio = jnp.log(nf / exact) / math.log(REL_MAX_DIST / exact)
    large = jnp.minimum(exact + (log_ratio * (REL_BUCKETS - exact)).astype(jnp.int32),
                        REL_BUCKETS - 1)
    return jnp.where(n < exact, n, large)


def _bias_kernel(rel_ref, out_ref, *, tile, col_stride, offset, limit, sub_far, keys_on_rows):
    m = pl.program_id(0)
    r = lax.broadcasted_iota(jnp.int32, (tile, tile), 0)
    c = lax.broadcasted_iota(jnp.int32, (tile, tile), 1)
    if keys_on_rows:
        dist = m * tile + c - r
    else:
        dist = m * tile + r - c * col_stride + offset
    valid = dist >= 0
    if limit is not None:
        valid = valid & (dist < limit)
    bucket = _t5_bucket(dist)
    for h in range(NSA_HEADS):
        val = jnp.zeros((tile, tile), F32)
        for b in range(REL_BUCKETS):
            val = jnp.where(bucket == b, rel_ref[b, h], val)
        if sub_far:
            val = val - rel_ref[REL_BUCKETS - 1, h]
        val = jnp.where(valid, val, NEG)
        if keys_on_rows:
            hh = h % NSA_HPG
            out_ref[0, h // NSA_HPG, :, hh * tile:(hh + 1) * tile] = val
        else:
            out_ref[0, h] = val


def _bias_table(rel, n_tiles, tile, *, col_stride=1, offset=0, limit=None, sub_far=False,
                keys_on_rows=False):
    shape = ((NSA_GROUPS, tile, NSA_HPG * tile) if keys_on_rows else (NSA_HEADS, tile, tile))
    return pl.pallas_call(
        functools.partial(_bias_kernel, tile=tile, col_stride=col_stride, offset=offset,
                          limit=limit, sub_far=sub_far, keys_on_rows=keys_on_rows),
        grid=(n_tiles,),
        in_specs=[pl.BlockSpec(memory_space=pltpu.SMEM)],
        out_specs=pl.BlockSpec((1,) + shape, lambda m: (m, 0, 0, 0)),
        out_shape=jax.ShapeDtypeStruct((n_tiles,) + shape, F32),
        compiler_params=_cparams(1),
        name="bias_table",
    )(rel)


def _ada_kernel(c_ref, w_ref, b_ref, o_ref):
    c = c_ref[...]
    cond = c * _sigmoid(c)
    o_ref[...] = jnp.dot(cond, w_ref[...], precision=HIGHEST,
                         preferred_element_type=F32) + b_ref[...]


def _ada(c, w, b):
    bsz = c.shape[0]
    n = w.shape[1]
    tn = 1024
    return pl.pallas_call(
        _ada_kernel,
        grid=(n // tn,),
        in_specs=[pl.BlockSpec((bsz, D_MODEL), lambda j: (0, 0)),
                  pl.BlockSpec((D_MODEL, tn), lambda j: (0, j)),
                  pl.BlockSpec((1, tn), lambda j: (0, j))],
        out_specs=pl.BlockSpec((bsz, tn), lambda j: (0, j)),
        out_shape=jax.ShapeDtypeStruct((bsz, n), F32),
        compiler_params=_cparams(1),
        name="ada_mod",
    )(c, w, b.reshape(1, n))


def _modulated_norm(x, gain, shift, scale):
    ms = jnp.mean(x * x, axis=-1, keepdims=True)
    return (x * lax.rsqrt(ms + RMS_EPS) * gain) * (1.0 + scale) + shift


def _inproj_kernel(x_ref, mod_ref, g_ref, w_ref, z_ref, cmp_ref, *, tn):
    mod = mod_ref[0]
    h = _modulated_norm(x_ref[0], g_ref[...], mod[:, 0:D_MODEL],
                        mod[:, D_MODEL:2 * D_MODEL])
    hb = h.astype(BF16)
    for n0 in range(0, Z_WIDTH, tn):
        z_ref[0, :, n0:n0 + tn] = jnp.dot(
            hb, w_ref[:, n0:n0 + tn], preferred_element_type=F32).astype(BF16)
    acc = jnp.dot(hb, w_ref[:, Z_WIDTH:Z_WIDTH + CMP_WIDTH], preferred_element_type=F32)
    for j in range(2 * NSA_GROUPS):
        cmp_ref[0, j] = acc[:, j * NSA_D:(j + 1) * NSA_D].astype(BF16)


def _inproj(x, mod3, gain, w_p, tm=512):
    bsz, seq, _ = x.shape
    ncol = w_p.shape[1]
    return pl.pallas_call(
        functools.partial(_inproj_kernel, tn=512),
        grid=(bsz, seq // tm),
        in_specs=[pl.BlockSpec((1, tm, D_MODEL), lambda b, i: (b, i, 0)),
                  pl.BlockSpec((1, 1, 6 * D_MODEL), lambda b, i: (b, 0, 0)),
                  pl.BlockSpec((1, D_MODEL), lambda b, i: (0, 0)),
                  pl.BlockSpec((D_MODEL, ncol), lambda b, i: (0, 0))],
        out_specs=[pl.BlockSpec((1, tm, Z_WIDTH), lambda b, i: (b, i, 0)),
                   pl.BlockSpec((1, 2 * NSA_GROUPS, tm, NSA_D), lambda b, i: (b, 0, i, 0))],
        out_shape=[jax.ShapeDtypeStruct((bsz, seq, Z_WIDTH), BF16),
                   jax.ShapeDtypeStruct((bsz, 2 * NSA_GROUPS, seq, NSA_D), BF16)],
        compiler_params=_cparams(2),
        name="in_proj",
    )(x, mod3, gain, w_p)


def _gelu_tanh(x):
    return 0.5 * x * (1.0 + jnp.tanh(math.sqrt(2.0 / math.pi) * (x + 0.044715 * (x * x * x))))


def _compress_kernel(y_ref, pos_ref, w1_ref, w2_ref, o_ref):
    y = y_ref[0, 0]
    w1 = w1_ref[0]
    half = (CMP_LEN // 2) * NSA_D
    top = jnp.dot(y, w1[:half].astype(BF16), preferred_element_type=F32)
    bot = jnp.dot(y, w1[half:].astype(BF16), preferred_element_type=F32)
    pos = jnp.broadcast_to(pos_ref[0], (8, CMP_LEN * NSA_D))
    pterm = jnp.dot(pos, w1, precision=HIGHEST, preferred_element_type=F32)[0:1]
    hidden = top + pltpu.roll(bot, N_CMP_PAD - 1, 0) + pterm
    act = _gelu_tanh(hidden).astype(BF16)
    o_ref[0, 0] = jnp.dot(act, w2_ref[0].astype(BF16),
                          preferred_element_type=F32).astype(BF16)


def _compress(cmp_in, pos, w1, w2):
    bsz, nkg, seq, _ = cmp_in.shape
    nhb = seq // CMP_STRIDE
    y = cmp_in.reshape(bsz, nkg, nhb, CMP_STRIDE * NSA_D)
    return pl.pallas_call(
        _compress_kernel,
        grid=(bsz, nkg),
        in_specs=[pl.BlockSpec((1, 1, nhb, CMP_STRIDE * NSA_D), lambda b, j: (b, j, 0, 0)),
                  pl.BlockSpec((1, 1, CMP_LEN * NSA_D), lambda b, j: (j // NSA_GROUPS, 0, 0)),
                  pl.BlockSpec((1, CMP_LEN * NSA_D, CMP_HIDDEN), lambda b, j: (j // NSA_GROUPS, 0, 0)),
                  pl.BlockSpec((1, CMP_HIDDEN, NSA_D), lambda b, j: (j // NSA_GROUPS, 0, 0))],
        out_specs=pl.BlockSpec((1, 1, nhb, NSA_D), lambda b, j: (b, j, 0, 0)),
        out_shape=jax.ShapeDtypeStruct((bsz, nkg, nhb, NSA_D), BF16),
        compiler_params=_cparams(2),
        name="compress",
    )(y, pos, w1, w2)


def _cmp_attn_kernel(q_ref, misc_ref, kc_ref, vc_ref, bias_ref, ovl_ref, o_ref, mb_ref, *, tq):
    i = pl.program_id(1)
    gates = _sigmoid(misc_ref[0].astype(F32))
    eye = (lax.broadcasted_iota(jnp.int32, (tq, tq), 0)
           == lax.broadcasted_iota(jnp.int32, (tq, tq), 1)).astype(BF16)
    jj = lax.broadcasted_iota(jnp.int32, (N_SEL_BLOCKS, tq), 0)
    tt = i * tq + lax.broadcasted_iota(jnp.int32, (N_SEL_BLOCKS, tq), 1)
    cur = tt // SEL_BLOCK
    forced = (jj == 0) | (jj == cur) | (jj == cur - 1)
    for g in range(NSA_GROUPS):
        kc = kc_ref[0, g]
        vc = vc_ref[0, g]
        psum = jnp.zeros((tq, N_CMP_PAD), F32)
        for hh in range(NSA_HPG):
            h = g * NSA_HPG + hh
            qh = q_ref[0, :, h * NSA_D:(h + 1) * NSA_D]
            b = bias_ref[h]
            s = lax.dot_general(qh, kc, _NT, preferred_element_type=F32) + b
            m = jnp.max(s, axis=-1, keepdims=True)
            e = jnp.exp(s - m)
            p = jnp.where(b > 0.5 * NEG, e / jnp.sum(e, axis=-1, keepdims=True), 0.0)
            psum = psum + p
            o = jnp.dot(p.astype(BF16), vc, preferred_element_type=F32)
            lane = GATE_LANE0 + 3 * h
            o_ref[0, :, h * NSA_D:(h + 1) * NSA_D] = (o * gates[:, lane:lane + 1]).astype(BF16)
        score = lax.dot_general(ovl_ref[...], psum, _NT, precision=HIGHEST,
                                preferred_element_type=F32)
        score = jnp.where(forced, jnp.inf, jnp.where(jj > cur, -jnp.inf, score))
        rank = jnp.zeros((N_SEL_BLOCKS, tq), jnp.int32)
        for j2 in range(N_SEL_BLOCKS):
            row = score[j2:j2 + 1, :]
            beats = (row > score) | ((row == score) & (jj > j2))
            rank = rank + beats.astype(jnp.int32)
        sel_t = jnp.where(rank < SEL_TOPK, 1.0, 0.0).astype(BF16)
        sel = lax.dot_general(eye, sel_t, _NT, preferred_element_type=F32)
        mb_ref[0, :, g * N_SEL_BLOCKS:(g + 1) * N_SEL_BLOCKS] = jnp.where(
            sel > 0.5, 0.0, NEG).astype(BF16)


def _cmp_attn(z, kvc, bias_c, ovl_t, tq=256):
    bsz, seq, _ = z.shape
    return pl.pallas_call(
        functools.partial(_cmp_attn_kernel, tq=tq),
        grid=(bsz, seq // tq),
        in_specs=[pl.BlockSpec((1, tq, D_MODEL), lambda b, i: (b, i, ZC_Q // D_MODEL)),
                  pl.BlockSpec((1, tq, LANES), lambda b, i: (b, i, ZC_MISC // LANES)),
                  pl.BlockSpec((1, NSA_GROUPS, N_CMP_PAD, NSA_D), lambda b, i: (b, 0, 0, 0)),
                  pl.BlockSpec((1, NSA_GROUPS, N_CMP_PAD, NSA_D), lambda b, i: (b, 1, 0, 0)),
                  pl.BlockSpec((NSA_HEADS, tq, N_CMP_PAD), lambda b, i: (0, i, 0)),
                  pl.BlockSpec((N_SEL_BLOCKS, N_CMP_PAD), lambda b, i: (0, 0))],
        out_specs=[pl.BlockSpec((1, tq, D_MODEL), lambda b, i: (b, i, 0)),
                   pl.BlockSpec((1, tq, LANES), lambda b, i: (b, i, 0))],
        out_shape=[jax.ShapeDtypeStruct((bsz, seq, D_MODEL), BF16),
                   jax.ShapeDtypeStruct((bsz, seq, LANES), BF16)],
        compiler_params=_cparams(2),
        name="cmp_attn_select",
    )(z, z, kvc, kvc, bias_c, ovl_t)


def _flash_step(s, vt, m, l, acc):
    m_new = jnp.maximum(m, jnp.max(s, axis=0, keepdims=True))
    alpha = jnp.exp(m - m_new)
    p = jnp.exp(s - m_new)
    l_new = alpha * l + jnp.sum(p, axis=0, keepdims=True)
    acc_new = alpha * acc + jnp.dot(vt, p.astype(BF16), preferred_element_type=F32)
    return m_new, l_new, acc_new


def _flash_init(d, n_queries=CHUNK):
    return (jnp.full((1, n_queries), NEG, F32), jnp.zeros((1, n_queries), F32),
            jnp.zeros((d, n_queries), F32))


def _flash_all(logits, vts, states):
    return tuple(_flash_step(s, vt, *st) for s, vt, st in zip(logits, vts, states))


def _flash_pipeline(first, last, logits_fn, vt_fn, states):
    def body(c, carry):
        s_cur, st = carry
        s_next = logits_fn(c + 1)
        return s_next, _flash_all(s_cur, vt_fn(c), st)

    return lax.fori_loop(first, last, body, (logits_fn(first), states))


def _nsa_flash_kernel(*refs, use_mask, n_back, gate_idx):
    if use_mask:
        (q_ref, mb_ref, misc_ref, k_ref, v_ref, bias_ref, o_ref,
         kaug, vt, qs_scr, gt_scr, ot_scr) = refs
    else:
        (q_ref, misc_ref, k_ref, v_ref, bias_ref, o_ref,
         kaug, vt, qs_scr, gt_scr, ot_scr) = refs
    i = pl.program_id(1)
    seq = k_ref.shape[1]

    @pl.when(i == 0)
    def _():
        if use_mask:
            blk = lax.broadcasted_iota(jnp.int32, (seq, NSA_D), 0) // SEL_BLOCK
            lane = lax.broadcasted_iota(jnp.int32, (seq, NSA_D), 1)
            extra = jnp.where(blk == lane, 1.0, 0.0).astype(BF16)
        else:
            extra = jnp.zeros((seq, NSA_D), BF16)
        eye = (lax.broadcasted_iota(jnp.int32, (NSA_D, NSA_D), 0)
               == lax.broadcasted_iota(jnp.int32, (NSA_D, NSA_D), 1)).astype(BF16)
        for g in range(NSA_GROUPS):
            gs = slice(g * NSA_D, (g + 1) * NSA_D)
            kaug[g] = jnp.concatenate([k_ref[0, :, gs], extra], axis=1)
            for c in range(seq // CHUNK):
                vt[g, c] = lax.dot_general(eye, v_ref[0, c * CHUNK:(c + 1) * CHUNK, gs], _NT,
                                           preferred_element_type=F32).astype(BF16)

    gt_scr[...] = _sigmoid(misc_ref[0].astype(F32)).T
    for h in range(NSA_HEADS):
        g, hh = divmod(h, NSA_HPG)
        parts = [q_ref[0, :, h * NSA_D:(h + 1) * NSA_D]]
        if use_mask:
            parts.append(mb_ref[0, :, g * N_SEL_BLOCKS:(g + 1) * N_SEL_BLOCKS])
        parts.append(jnp.zeros((CHUNK, LANES - sum(p.shape[1] for p in parts)), BF16))
        qs_scr[g, hh * CHUNK:(hh + 1) * CHUNK, :] = jnp.concatenate(parts, axis=1)

    lo = 0 if n_back is None else jnp.maximum(i - n_back, 0)

    def group_body(g, carry):
        qs = qs_scr[g]

        def logits(c):
            off = pl.multiple_of(c * CHUNK, CHUNK)
            s = lax.dot_general(kaug[g, pl.ds(off, CHUNK), :], qs, _NT,
                                preferred_element_type=F32)
            return (s + bias_ref[jnp.minimum(i - c, 2), g],)

        s_last, states = _flash_pipeline(lo, i, logits, lambda c: (vt[g, c],),
                                         (_flash_init(NSA_D, NSA_HPG * CHUNK),))
        (m, l, acc), = _flash_all(s_last, (vt[g, i],), states)
        h0 = g * NSA_HPG
        gate = jnp.concatenate(
            [gt_scr[pl.ds(GATE_LANE0 + 3 * (h0 + hh) + gate_idx, 1), :]
             for hh in range(NSA_HPG)], axis=1)
        out = acc * (gate / l)
        for hh in range(NSA_HPG):
            row0 = pl.multiple_of((h0 + hh) * NSA_D, NSA_D)
            ot_scr[pl.ds(row0, NSA_D), :] = out[:, hh * CHUNK:(hh + 1) * CHUNK]
        return carry

    lax.fori_loop(0, NSA_GROUPS, group_body, 0)
    o_ref[0] = ot_scr[...].T.astype(BF16)


def _nsa_flash(z, mb, bias_t, *, k_col, v_col, n_back, gate_idx):
    bsz, seq, _ = z.shape
    use_mask = mb is not None
    in_specs = [pl.BlockSpec((1, CHUNK, D_MODEL), lambda b, i: (b, i, ZC_Q // D_MODEL))]
    args = [z]
    if use_mask:
        in_specs.append(pl.BlockSpec((1, CHUNK, LANES), lambda b, i: (b, i, 0)))
        args.append(mb)
    in_specs += [pl.BlockSpec((1, CHUNK, LANES), lambda b, i: (b, i, ZC_MISC // LANES)),
                 pl.BlockSpec((1, seq, 256), lambda b, i: (b, 0, k_col // 256)),
                 pl.BlockSpec((1, seq, 256), lambda b, i: (b, 0, v_col // 256)),
                 pl.BlockSpec((3, NSA_GROUPS, CHUNK, NSA_HPG * CHUNK), lambda b, i: (0, 0, 0, 0),
                              pipeline_mode=pl.Buffered(1))]
    args += [z, z, z, bias_t]
    return pl.pallas_call(
        functools.partial(_nsa_flash_kernel, use_mask=use_mask, n_back=n_back,
                          gate_idx=gate_idx),
        grid=(bsz, seq // CHUNK),
        in_specs=in_specs,
        out_specs=pl.BlockSpec((1, CHUNK, D_MODEL), lambda b, i: (b, i, 0)),
        out_shape=jax.ShapeDtypeStruct((bsz, seq, D_MODEL), BF16),
        scratch_shapes=[pltpu.VMEM((NSA_GROUPS, seq, LANES), BF16),
                        pltpu.VMEM((NSA_GROUPS, seq // CHUNK, NSA_D, CHUNK), BF16),
                        pltpu.VMEM((NSA_GROUPS, NSA_HPG * CHUNK, LANES), BF16),
                        pltpu.VMEM((LANES, CHUNK), F32),
                        pltpu.VMEM((D_MODEL, CHUNK), F32)],
        compiler_params=_cparams(2),
        name="slc_attn" if use_mask else "win_attn",
    )(*args)


def _rms(x, gain):
    ms = jnp.mean(x * x, axis=-1, keepdims=True)
    return x * lax.rsqrt(ms + RMS_EPS) * gain


def _mla_prep_kernel(cq_ref, ckv_ref, misc_ref, pos_ref, qg_ref, kvg_ref, wqa_ref, wqb_ref,
                     wk_ref, wvt_ref, pa_ref, pb_ref, frq_ref, q_out, k_out, vt_out):
    tm = cq_ref.shape[1]
    cqn = _rms(cq_ref[0].astype(F32), qg_ref[...]).astype(BF16)
    ckvn = _rms(ckv_ref[0].astype(F32), kvg_ref[...]).astype(BF16)
    ang = pos_ref[0].astype(F32) * frq_ref[...]
    cos = jnp.cos(ang)
    sin = jnp.sin(ang)
    scale = (MLA_NOPE + MLA_ROPE) ** -0.5
    qa = jnp.dot(cqn, wqa_ref[...], preferred_element_type=F32)
    qb = jnp.dot(cqn, wqb_ref[...], preferred_element_type=F32)
    misc = misc_ref[0]
    kr = (jnp.dot(misc, pa_ref[...], preferred_element_type=F32) * cos
          + jnp.dot(misc, pb_ref[...], preferred_element_type=F32) * sin)
    kn = jnp.dot(ckvn, wk_ref[...], preferred_element_type=F32)
    vt = lax.dot_general(wvt_ref[...], ckvn, _NT,
                         preferred_element_type=F32).astype(BF16)
    for h in range(MLA_HEADS):
        sl = slice(h * LANES, (h + 1) * LANES)
        q_out[0, h] = ((qa[:, sl] * cos + qb[:, sl] * sin) * scale).astype(BF16)
        k_out[0, h] = (kn[:, sl] + kr).astype(BF16)
        for cc in range(tm // CHUNK):
            vt_out[0, h, cc] = vt[sl, cc * CHUNK:(cc + 1) * CHUNK]


def _mla_prep(z, pos3, qg, kvg, wqa, wqb, wk, wvt, pa, pb, frq, tm=512):
    bsz, seq, _ = z.shape
    full = lambda shape: pl.BlockSpec(shape, lambda b, i: (0,) * len(shape))
    qk_shape = jax.ShapeDtypeStruct((bsz, MLA_HEADS, seq, LANES), BF16)
    qk_spec = pl.BlockSpec((1, MLA_HEADS, tm, LANES), lambda b, i: (b, 0, i, 0))
    return pl.pallas_call(
        _mla_prep_kernel,
        grid=(bsz, seq // tm),
        in_specs=[pl.BlockSpec((1, tm, MLA_Q_RANK), lambda b, i: (b, i, ZC_CQ // MLA_Q_RANK)),
                  pl.BlockSpec((1, tm, MLA_KV_RANK), lambda b, i: (b, i, ZC_CKV // MLA_KV_RANK)),
                  pl.BlockSpec((1, tm, LANES), lambda b, i: (b, i, ZC_MISC // LANES)),
                  pl.BlockSpec((1, tm, 1), lambda b, i: (b, i, 0)),
                  full((1, MLA_Q_RANK)), full((1, MLA_KV_RANK)),
                  full((MLA_Q_RANK, MLA_HEADS * LANES)), full((MLA_Q_RANK, MLA_HEADS * LANES)),
                  full((MLA_KV_RANK, MLA_HEADS * LANES)), full((MLA_HEADS * MLA_V, MLA_KV_RANK)),
                  full((LANES, LANES)), full((LANES, LANES)), full((1, LANES))],
        out_specs=[qk_spec, qk_spec,
                   pl.BlockSpec((1, MLA_HEADS, tm // CHUNK, MLA_V, CHUNK),
                                lambda b, i: (b, 0, i, 0, 0))],
        out_shape=[qk_shape, qk_shape,
                   jax.ShapeDtypeStruct((bsz, MLA_HEADS, seq // CHUNK, MLA_V, CHUNK), BF16)],
        compiler_params=_cparams(2),
        name="mla_prep",
    )(z, z, z, pos3, qg, kvg, wqa, wqb, wk, wvt, pa, pb, frq)


def _mla_attn_kernel(q_ref, k_ref, vt_ref, o_ref, ot_scr):
    i = pl.program_id(1)
    causal = (lax.broadcasted_iota(jnp.int32, (CHUNK, CHUNK), 0)
              <= lax.broadcasted_iota(jnp.int32, (CHUNK, CHUNK), 1))

    def heads_body(hb, carry):
        heads = [hb * MLA_HEADS_PER_TRIP + j for j in range(MLA_HEADS_PER_TRIP)]
        qs = [q_ref[0, h] for h in heads]

        def logits(c):
            off = pl.multiple_of(c * CHUNK, CHUNK)
            return tuple(lax.dot_general(k_ref[0, h, pl.ds(off, CHUNK), :], q, _NT,
                                         preferred_element_type=F32)
                         for h, q in zip(heads, qs))

        def vts(c):
            return tuple(vt_ref[0, h, c] for h in heads)

        prev = jnp.maximum(i - 1, 0)
        s_prev, states = _flash_pipeline(0, prev, logits, vts,
                                         tuple(_flash_init(MLA_V) for _ in heads))
        s_diag = tuple(jnp.where(causal, s, NEG) for s in logits(i))
        dummy = jnp.where(i > 0, 0.0, NEG)
        states = _flash_all(tuple(s + dummy for s in s_prev), vts(prev), states)
        states = _flash_all(s_diag, vts(i), states)
        for h, (m, l, acc) in zip(heads, states):
            ot_scr[pl.ds(pl.multiple_of(h * MLA_V, MLA_V), MLA_V), :] = acc / l
        return carry

    lax.fori_loop(0, MLA_HEADS // MLA_HEADS_PER_TRIP, heads_body, 0)
    o_ref[0] = ot_scr[...].T.astype(BF16)


def _mla_attn(q, k, vt):
    bsz, nh, seq, _ = q.shape
    return pl.pallas_call(
        _mla_attn_kernel,
        grid=(bsz, seq // CHUNK),
        in_specs=[pl.BlockSpec((1, nh, CHUNK, LANES), lambda b, i: (b, 0, i, 0)),
                  pl.BlockSpec((1, nh, seq, LANES), lambda b, i: (b, 0, 0, 0)),
                  pl.BlockSpec((1, nh, seq // CHUNK, MLA_V, CHUNK), lambda b, i: (b, 0, 0, 0, 0))],
        out_specs=pl.BlockSpec((1, CHUNK, nh * MLA_V), lambda b, i: (b, i, 0)),
        out_shape=jax.ShapeDtypeStruct((bsz, seq, nh * MLA_V), BF16),
        scratch_shapes=[pltpu.VMEM((nh * MLA_V, CHUNK), F32)],
        compiler_params=_cparams(2),
        name="mla_attn",
    )(q, k, vt)


def _merge_kernel(x_ref, mod_ref, oc_ref, os_ref, ow_ref, om_ref, ma_ref, mb_ref, wo_ref,
                  g_ref, x1_ref, h2_ref):
    mod = mod_ref[0]
    o_nsa = oc_ref[0].astype(F32) + os_ref[0].astype(F32) + ow_ref[0].astype(F32)
    y = (_sigmoid(ma_ref[0].astype(F32)) * o_nsa
         + _sigmoid(mb_ref[0].astype(F32)) * om_ref[0].astype(F32))
    a = jnp.dot(y.astype(BF16), wo_ref[...], preferred_element_type=F32)
    x1 = x_ref[0] + mod[:, 2 * D_MODEL:3 * D_MODEL] * a
    x1_ref[0] = x1
    h2 = _modulated_norm(x1, g_ref[...], mod[:, 3 * D_MODEL:4 * D_MODEL],
                         mod[:, 4 * D_MODEL:5 * D_MODEL])
    h2_ref[0] = h2.astype(BF16)


def _merge(x, mod3, o_cmp, o_slc, o_win, o_mla, z, w_o, gain, tm=512):
    bsz, seq, _ = x.shape
    tok = lambda col: pl.BlockSpec((1, tm, D_MODEL), lambda b, i: (b, i, col))
    return pl.pallas_call(
        _merge_kernel,
        grid=(bsz, seq // tm),
        in_specs=[tok(0),
                  pl.BlockSpec((1, 1, 6 * D_MODEL), lambda b, i: (b, 0, 0)),
                  tok(0), tok(0), tok(0), tok(0),
                  tok(ZC_MA // D_MODEL), tok(ZC_MB // D_MODEL),
                  pl.BlockSpec((D_MODEL, D_MODEL), lambda b, i: (0, 0)),
                  pl.BlockSpec((1, D_MODEL), lambda b, i: (0, 0))],
        out_specs=[tok(0), tok(0)],
        out_shape=[jax.ShapeDtypeStruct((bsz, seq, D_MODEL), F32),
                   jax.ShapeDtypeStruct((bsz, seq, D_MODEL), BF16)],
        compiler_params=_cparams(2),
        name="merge_out_proj",
    )(x, mod3, o_cmp, o_slc, o_win, o_mla, z, z, w_o, gain)


def _ffn_kernel(x1_ref, h2_ref, mod_ref, wg_ref, wu_ref, cw_ref, cb_ref, wd_ref, fg_ref,
                o_ref, prev_ref, *, tm, tf):
    i = pl.program_id(1)

    @pl.when(i == 0)
    def _():
        prev_ref[...] = jnp.zeros_like(prev_ref)

    h2 = h2_ref[0]
    row = lax.broadcasted_iota(jnp.int32, (tm, tf), 0)
    acc = jnp.zeros((tm, D_MODEL), F32)
    for f0 in range(0, D_FF, tf):
        fs = slice(f0, f0 + tf)
        gt = jnp.dot(h2, wg_ref[:, fs], preferred_element_type=F32)
        p1 = prev_ref[7:8, fs]
        p2 = prev_ref[6:7, fs]
        g1 = jnp.where(row == 0, p1, pltpu.roll(gt, 1, 0))
        g2 = jnp.where(row == 0, p2, jnp.where(row == 1, p1, pltpu.roll(gt, 2, 0)))
        prev_ref[:, fs] = gt[tm - 8:tm, :]
        conv = cb_ref[:, fs] + cw_ref[0:1, fs] * g2 + cw_ref[1:2, fs] * g1 + cw_ref[2:3, fs] * gt
        up = jnp.dot(h2, wu_ref[:, fs], preferred_element_type=F32)
        act = (conv * _sigmoid(conv)) * up
        acc = acc + jnp.dot(act.astype(BF16), wd_ref[fs, :], preferred_element_type=F32)
    mod = mod_ref[0]
    x2 = x1_ref[0] + mod[:, 5 * D_MODEL:6 * D_MODEL] * acc
    ms = jnp.mean(x2 * x2, axis=-1, keepdims=True)
    o_ref[0] = x2 * lax.rsqrt(ms + RMS_EPS) * fg_ref[...]


def _ffn(x1, h2, mod3, wg, wu, cw, cb, wd, fg, tm=512, tf=1408):
    bsz, seq, _ = x1.shape
    tok = pl.BlockSpec((1, tm, D_MODEL), lambda b, i: (b, i, 0))
    const = lambda shape: pl.BlockSpec(shape, lambda b, i: (0,) * len(shape),
                                       pipeline_mode=pl.Buffered(1))
    return pl.pallas_call(
        functools.partial(_ffn_kernel, tm=tm, tf=tf),
        grid=(bsz, seq // tm),
        in_specs=[tok, tok,
                  pl.BlockSpec((1, 1, 6 * D_MODEL), lambda b, i: (b, 0, 0)),
                  const((D_MODEL, D_FF)), const((D_MODEL, D_FF)),
                  const((3, D_FF)), const((1, D_FF)), const((D_FF, D_MODEL)),
                  const((1, D_MODEL))],
        out_specs=tok,
        out_shape=jax.ShapeDtypeStruct((bsz, seq, D_MODEL), F32),
        scratch_shapes=[pltpu.VMEM((8, D_FF), F32)],
        compiler_params=_cparams(2),
        name="conv_glu_ffn",
    )(x1, h2, mod3, wg, wu, cw, cb, wd, fg)


def _prep_w_in(w_in):
    off = {}
    o = 0
    for name, width in (("nsa_q", 1024), ("k_cmp", 256), ("v_cmp", 256), ("k_slc", 256),
                        ("v_slc", 256), ("k_win", 256), ("v_win", 256), ("nsa_gate", 48),
                        ("mla_cq", 256), ("mla_ckv", 128), ("mla_krope", 32),
                        ("merge_a", 1024), ("merge_b", 1024)):
        off[name] = (o, o + width)
        o += width
    col = lambda name: w_in[:, off[name][0]:off[name][1]]
    pad = jnp.zeros((D_MODEL, LANES - MLA_ROPE - 3 * NSA_HEADS), w_in.dtype)
    parts = [col("nsa_q") * (NSA_D ** -0.5),
             col("merge_a"), col("merge_b"), col("k_slc"), col("v_slc"), col("k_win"),
             col("v_win"), col("mla_cq"), col("mla_ckv"), col("mla_krope"), col("nsa_gate"),
             pad, col("k_cmp"), col("v_cmp")]
    return jnp.concatenate(parts, axis=1).astype(BF16)


def _prep_mla_weights(w_uq, w_ukv):
    half = MLA_ROPE // 2
    dq = MLA_NOPE + MLA_ROPE
    wq = w_uq.reshape(MLA_Q_RANK, MLA_HEADS, dq)
    nope, x1, x2 = wq[..., :MLA_NOPE], wq[..., MLA_NOPE:MLA_NOPE + half], wq[..., MLA_NOPE + half:]
    z32 = jnp.zeros((MLA_Q_RANK, MLA_HEADS, LANES - dq), w_uq.dtype)
    wqa = jnp.concatenate([nope, x1, x2, z32], axis=-1)
    wqb = jnp.concatenate([jnp.zeros_like(nope), -x2, x1, z32], axis=-1)
    wkv = w_ukv.reshape(MLA_KV_RANK, MLA_HEADS, MLA_NOPE + MLA_V)
    wk = jnp.concatenate([wkv[..., :MLA_NOPE],
                          jnp.zeros((MLA_KV_RANK, MLA_HEADS, LANES - MLA_NOPE), w_ukv.dtype)],
                         axis=-1)
    wv = wkv[..., MLA_NOPE:]
    flat = lambda w: w.reshape(w.shape[0], MLA_HEADS * LANES).astype(BF16)
    return flat(wqa), flat(wqb), flat(wk), flat(wv).T


def _rope_constants():
    half = MLA_ROPE // 2
    pa = np.zeros((LANES, LANES), np.float32)
    pb = np.zeros((LANES, LANES), np.float32)
    for j in range(half):
        pa[j, MLA_NOPE + j] = 1.0
        pa[half + j, MLA_NOPE + half + j] = 1.0
        pb[half + j, MLA_NOPE + j] = -1.0
        pb[j, MLA_NOPE + half + j] = 1.0
    inv_freq = ROPE_THETA ** (-jnp.arange(0, MLA_ROPE, 2, dtype=F32) / MLA_ROPE)
    frq = jnp.zeros((1, LANES), F32)
    frq = frq.at[0, MLA_NOPE:MLA_NOPE + half].set(inv_freq)
    frq = frq.at[0, MLA_NOPE + half:MLA_NOPE + 2 * half].set(inv_freq)
    return jnp.asarray(pa, BF16), jnp.asarray(pb, BF16), frq


def _overlap_t(seq):
    nc = (seq - CMP_LEN) // CMP_STRIDE + 1
    nb = seq // SEL_BLOCK
    cs = np.arange(nc) * CMP_STRIDE
    bs = np.arange(nb) * SEL_BLOCK
    ov = np.clip(np.minimum(cs[:, None] + CMP_LEN, bs[None, :] + SEL_BLOCK)
                 - np.maximum(cs[:, None], bs[None, :]), 0, None) / CMP_LEN
    out = np.zeros((nb, N_CMP_PAD), np.float32)
    out[:, :nc] = ov.T
    return jnp.asarray(out)


def kernel(x, c, positions, rel_bias_table, ada_w, ada_b, norm_mix_g, w_in, cmp_pos_k, cmp_w1_k, cmp_w2_k, cmp_pos_v, cmp_w1_v, cmp_w2_v, mla_q_norm_g, mla_w_uq, mla_kv_norm_g, mla_w_ukv, w_o, norm_ffn_g, ffn_w_gate, ffn_w_up, ffn_conv_w, ffn_conv_b, ffn_w_down, final_norm_g):
    bsz, seq, _ = x.shape
    assert ada_w.shape[0] == 1 and seq == N_SEL_BLOCKS * SEL_BLOCK
    n_back = WINDOW // CHUNK

    bias_c = _bias_table(rel_bias_table, seq // TILE, TILE, col_stride=CMP_STRIDE,
                         offset=-(CMP_LEN - 1))
    bias_c = bias_c.transpose(1, 0, 2, 3).reshape(NSA_HEADS, seq, N_CMP_PAD)
    bias_s = _bias_table(rel_bias_table, 3, CHUNK, sub_far=True, keys_on_rows=True)
    bias_w = _bias_table(rel_bias_table, n_back + 1, CHUNK, limit=WINDOW, keys_on_rows=True)

    mod3 = _ada(c, ada_w[0], ada_b[0]).reshape(bsz, 1, 6 * D_MODEL)
    z, cmp_in = _inproj(x, mod3, norm_mix_g, _prep_w_in(w_in[0]))

    kvc = _compress(cmp_in,
                    jnp.stack([cmp_pos_k[0].reshape(1, -1), cmp_pos_v[0].reshape(1, -1)]),
                    jnp.stack([cmp_w1_k[0], cmp_w1_v[0]]),
                    jnp.stack([cmp_w2_k[0], cmp_w2_v[0]]))

    o_cmp, mb = _cmp_attn(z, kvc, bias_c, _overlap_t(seq))
    o_slc = _nsa_flash(z, mb, bias_s, k_col=ZC_KS, v_col=ZC_VS, n_back=None, gate_idx=1)
    o_win = _nsa_flash(z, None, bias_w, k_col=ZC_KW, v_col=ZC_VW, n_back=n_back, gate_idx=2)

    wqa, wqb, wk, wvt = _prep_mla_weights(mla_w_uq[0], mla_w_ukv[0])
    pa, pb, frq = _rope_constants()
    q_m, k_m, vt_m = _mla_prep(z, positions.reshape(bsz, seq, 1), mla_q_norm_g, mla_kv_norm_g,
                               wqa, wqb, wk, wvt, pa, pb, frq)
    o_mla = _mla_attn(q_m, k_m, vt_m)

    x1, h2 = _merge(x, mod3, o_cmp, o_slc, o_win, o_mla, z, w_o[0].astype(BF16), norm_ffn_g)
    return _ffn(x1, h2, mod3, ffn_w_gate[0].astype(BF16), ffn_w_up[0].astype(BF16),
                ffn_conv_w[0], ffn_conv_b, ffn_w_down[0].astype(BF16),
                final_norm_g.reshape(1, D_MODEL))
```

```python
import functools
import math

import numpy as np
import jax
import jax.numpy as jnp
from jax import lax
from jax.experimental import pallas as pl
from jax.experimental.pallas import tpu as pltpu

F32 = jnp.float32
BF16 = jnp.bfloat16
HIGHEST = lax.Precision.HIGHEST

D_MODEL = 1024
NSA_HEADS = 16
NSA_GROUPS = 4
NSA_HPG = 4
NSA_D = 64
CMP_LEN = 32
CMP_STRIDE = 16
CMP_HIDDEN = 256
SEL_BLOCK = 64
SEL_TOPK = 16
WINDOW = 512
MLA_HEADS = 8
MLA_Q_RANK = 256
MLA_KV_RANK = 128
MLA_NOPE = 64
MLA_ROPE = 32
MLA_V = 128
ROPE_THETA = 10000.0
REL_BUCKETS = 32
REL_MAX_DIST = 128
D_FF = 2816
RMS_EPS = 1e-6
NEG = -1e30
LOG2E = math.log2(math.e)

LANES = 128
CHUNK = 256
N_CMP_PAD = 128
N_SEL_BLOCKS = 32
ONES_ROWS = 16

ZC_Q = 0
ZC_MA = 2048
ZC_MB = 3072
ZC_KS = 4096
ZC_VS = 4352
ZC_KW = 4608
ZC_VW = 4864
ZC_CQ = 5120
ZC_CKV = 5376
ZC_MISC = 5504
Z_WIDTH = 5632
Q_WIDTH = NSA_HEADS * LANES
CMP_WIDTH = 512
GATE_LANE0 = MLA_ROPE
MASK_LANE0 = NSA_D

VMEM_LIMIT = 56 * 1024 * 1024

_NT = (((1,), (1,)), ((), ()))


def _cparams(n_axes):
    return pltpu.CompilerParams(
        dimension_semantics=("arbitrary",) * n_axes,
        vmem_limit_bytes=VMEM_LIMIT)


def _sigmoid(x):
    return 1.0 / (1.0 + jnp.exp(-x))


def _const_spec(shape):
    return pl.BlockSpec(shape, lambda *_: (0,) * len(shape), pipeline_mode=pl.Buffered(1))


def _t5_bucket(dist):
    n = jnp.maximum(dist, 0)
    exact = REL_BUCKETS // 2
    nf = jnp.maximum(n, exact).astype(F32)
    log_ratio = jnp.log(nf / exact) / math.log(REL_MAX_DIST / exact)
    large = jnp.minimum(exact + (log_ratio * (REL_BUCKETS - exact)).astype(jnp.int32),
                        REL_BUCKETS - 1)
    return jnp.where(n < exact, n, large)


def _bias_kernel(rel_ref, out_ref, *, rows, cols, row_coef, offset, limit, sub_far,
                 group_lanes):
    m = pl.program_id(0)
    r = lax.broadcasted_iota(jnp.int32, (rows, cols), 0)
    c = lax.broadcasted_iota(jnp.int32, (rows, cols), 1)
    dist = m * cols + c + row_coef * r + offset
    valid = dist >= 0
    if limit is not None:
        valid = valid & (dist < limit)
    bucket = _t5_bucket(dist)
    for h in range(NSA_HEADS):
        val = jnp.zeros((rows, cols), F32)
        for b in range(REL_BUCKETS):
            val = jnp.where(bucket == b, rel_ref[b, h], val)
        if sub_far:
            val = val - rel_ref[REL_BUCKETS - 1, h]
        val = jnp.where(valid, val * LOG2E, NEG)
        if group_lanes:
            hh = h % NSA_HPG
            out_ref[0, h // NSA_HPG, :, hh * cols:(hh + 1) * cols] = val
        else:
            out_ref[h] = val


def _cmp_bias_table(rel, seq):
    return pl.pallas_call(
        functools.partial(_bias_kernel, rows=N_CMP_PAD, cols=CHUNK, row_coef=-CMP_STRIDE,
                          offset=-(CMP_LEN - 1), limit=None, sub_far=False, group_lanes=False),
        grid=(seq // CHUNK,),
        in_specs=[pl.BlockSpec(memory_space=pltpu.SMEM)],
        out_specs=pl.BlockSpec((NSA_HEADS, N_CMP_PAD, CHUNK), lambda m: (0, 0, m)),
        out_shape=jax.ShapeDtypeStruct((NSA_HEADS, N_CMP_PAD, seq), F32),
        compiler_params=_cparams(1),
        name="bias_cmp",
    )(rel)


def _chunk_bias_table(rel, *, limit, sub_far):
    return pl.pallas_call(
        functools.partial(_bias_kernel, rows=CHUNK, cols=CHUNK, row_coef=-1, offset=0,
                          limit=limit, sub_far=sub_far, group_lanes=True),
        grid=(3,),
        in_specs=[pl.BlockSpec(memory_space=pltpu.SMEM)],
        out_specs=pl.BlockSpec((1, NSA_GROUPS, CHUNK, NSA_HPG * CHUNK), lambda m: (m, 0, 0, 0)),
        out_shape=jax.ShapeDtypeStruct((3, NSA_GROUPS, CHUNK, NSA_HPG * CHUNK), F32),
        compiler_params=_cparams(1),
        name="bias_chunk",
    )(rel)


def _ada_kernel(c_ref, w_ref, b_ref, o_ref):
    c = c_ref[...]
    cond = c * _sigmoid(c)
    o_ref[...] = jnp.dot(cond, w_ref[...], precision=HIGHEST,
                         preferred_element_type=F32) + b_ref[...]


def _ada(c, w, b):
    bsz = c.shape[0]
    n = w.shape[1]
    tn = 1024
    return pl.pallas_call(
        _ada_kernel,
        grid=(n // tn,),
        in_specs=[pl.BlockSpec((bsz, D_MODEL), lambda j: (0, 0)),
                  pl.BlockSpec((D_MODEL, tn), lambda j: (0, j)),
                  pl.BlockSpec((1, tn), lambda j: (0, j))],
        out_specs=pl.BlockSpec((bsz, tn), lambda j: (0, j)),
        out_shape=jax.ShapeDtypeStruct((bsz, n), F32),
        compiler_params=_cparams(1),
        name="ada_mod",
    )(c, w, b.reshape(1, n))


def _modulated_norm(x, gain, shift, scale):
    ms = jnp.mean(x * x, axis=-1, keepdims=True)
    return (x * lax.rsqrt(ms + RMS_EPS) * gain) * (1.0 + scale) + shift


def _inproj_kernel(x_ref, mod_ref, g_ref, w_ref, z_ref, cmp_ref, *, tn):
    mod = mod_ref[0]
    h = _modulated_norm(x_ref[0], g_ref[...], mod[:, 0:D_MODEL],
                        mod[:, D_MODEL:2 * D_MODEL])
    hb = h.astype(BF16)
    for n0 in range(0, Z_WIDTH, tn):
        z_ref[0, :, n0:n0 + tn] = jnp.dot(
            hb, w_ref[:, n0:n0 + tn], preferred_element_type=F32).astype(BF16)
    acc = jnp.dot(hb, w_ref[:, Z_WIDTH:Z_WIDTH + CMP_WIDTH], preferred_element_type=F32)
    for j in range(2 * NSA_GROUPS):
        cmp_ref[0, j] = acc[:, j * NSA_D:(j + 1) * NSA_D].astype(BF16)


def _inproj(x, mod3, gain, w_p, tm=512):
    bsz, seq, _ = x.shape
    return pl.pallas_call(
        functools.partial(_inproj_kernel, tn=512),
        grid=(bsz, seq // tm),
        in_specs=[pl.BlockSpec((1, tm, D_MODEL), lambda b, i: (b, i, 0)),
                  pl.BlockSpec((1, 1, 6 * D_MODEL), lambda b, i: (b, 0, 0)),
                  pl.BlockSpec((1, D_MODEL), lambda b, i: (0, 0)),
                  _const_spec(w_p.shape)],
        out_specs=[pl.BlockSpec((1, tm, Z_WIDTH), lambda b, i: (b, i, 0)),
                   pl.BlockSpec((1, 2 * NSA_GROUPS, tm, NSA_D), lambda b, i: (b, 0, i, 0))],
        out_shape=[jax.ShapeDtypeStruct((bsz, seq, Z_WIDTH), BF16),
                   jax.ShapeDtypeStruct((bsz, 2 * NSA_GROUPS, seq, NSA_D), BF16)],
        compiler_params=_cparams(2),
        name="in_proj",
    )(x, mod3, gain, w_p)


def _gelu_tanh(x):
    return 0.5 * x * (1.0 + jnp.tanh(math.sqrt(2.0 / math.pi) * (x + 0.044715 * (x * x * x))))


def _compress_kernel(y_ref, pos_ref, w1_ref, w2_ref, w2t_ref, o_ref, ot_ref):
    y = y_ref[0, 0]
    w1 = w1_ref[0]
    half = (CMP_LEN // 2) * NSA_D
    top = jnp.dot(y, w1[:half].astype(BF16), preferred_element_type=F32)
    bot = jnp.dot(y, w1[half:].astype(BF16), preferred_element_type=F32)
    pos = jnp.broadcast_to(pos_ref[0], (8, CMP_LEN * NSA_D))
    pterm = jnp.dot(pos, w1, precision=HIGHEST, preferred_element_type=F32)[0:1]
    hidden = top + pltpu.roll(bot, N_CMP_PAD - 1, 0) + pterm
    act = _gelu_tanh(hidden).astype(BF16)
    out = jnp.dot(act, w2_ref[0], preferred_element_type=F32).astype(BF16)
    o_ref[0, 0] = jnp.concatenate([out, jnp.zeros_like(out)], axis=1)
    ot_ref[0, 0] = lax.dot_general(w2t_ref[0], act, _NT,
                                   preferred_element_type=F32).astype(BF16)


def _compress(cmp_in, pos, w1, w2):
    bsz, nkg, seq, _ = cmp_in.shape
    nhb = seq // CMP_STRIDE
    y = cmp_in.reshape(bsz, nkg, nhb, CMP_STRIDE * NSA_D)
    kv = lambda b, j: (j // NSA_GROUPS, 0, 0)
    w2b = w2.astype(BF16)
    return pl.pallas_call(
        _compress_kernel,
        grid=(bsz, nkg),
        in_specs=[pl.BlockSpec((1, 1, nhb, CMP_STRIDE * NSA_D), lambda b, j: (b, j, 0, 0)),
                  pl.BlockSpec((1, 1, CMP_LEN * NSA_D), kv),
                  pl.BlockSpec((1, CMP_LEN * NSA_D, CMP_HIDDEN), kv),
                  pl.BlockSpec((1, CMP_HIDDEN, NSA_D), kv),
                  pl.BlockSpec((1, NSA_D, CMP_HIDDEN), kv)],
        out_specs=[pl.BlockSpec((1, 1, nhb, LANES), lambda b, j: (b, j, 0, 0)),
                   pl.BlockSpec((1, 1, NSA_D, nhb), lambda b, j: (b, j, 0, 0))],
        out_shape=[jax.ShapeDtypeStruct((bsz, nkg, nhb, LANES), BF16),
                   jax.ShapeDtypeStruct((bsz, nkg, NSA_D, nhb), BF16)],
        compiler_params=_cparams(2),
        name="compress",
    )(y, pos, w1, w2b, w2b.transpose(0, 2, 1))


def _select_mask(score, jj, cur):
    forced = (jj == 0) | (jj == cur) | (jj == cur - 1)
    x = jnp.where(forced, jnp.inf, jnp.where(jj > cur, -jnp.inf, score))
    below = pltpu.bitcast(pltpu.bitcast(x, jnp.int32) - 1, F32)
    x_lo = jnp.where(x > 0.0, below, jnp.where(x == 0.0, -1.0, x))
    tq = x.shape[1]
    n_sub = 8
    masks = []
    for v in range(N_SEL_BLOCKS // n_sub):
        rows = slice(v * n_sub, (v + 1) * n_sub)
        xv, xlv = x[rows], x_lo[rows]
        jv = v * n_sub + lax.broadcasted_iota(jnp.int32, (n_sub, tq), 0)
        rank = jnp.zeros((n_sub, tq), F32)
        for j2 in range(N_SEL_BLOCKS):
            row = x[j2:j2 + 1, :]
            if v * n_sub > j2:
                thr = xlv
            elif (v + 1) * n_sub - 1 <= j2:
                thr = xv
            else:
                thr = jnp.where(jv > j2, xlv, xv)
            rank = rank + jnp.where(row > thr, 1.0, 0.0)
        masks.append(jnp.where(rank < SEL_TOPK, 0.0, NEG))
    return jnp.concatenate(masks, axis=0)


def _cmp_attn_kernel(q_ref, misc_ref, kc_ref, vct_ref, bias_ref, ovl_ref, o_ref, mb_ref,
                     gt_scr, ot_scr, mbt_scr):
    i = pl.program_id(1)
    tq = CHUNK
    gt_scr[...] = _sigmoid(misc_ref[0].astype(F32)).T
    jj = lax.broadcasted_iota(jnp.int32, (N_SEL_BLOCKS, tq), 0)
    tt = i * tq + lax.broadcasted_iota(jnp.int32, (N_SEL_BLOCKS, tq), 1)
    cur = tt // SEL_BLOCK
    for g in range(NSA_GROUPS):
        kc = kc_ref[0, g]
        vct = vct_ref[0, g]
        psum = jnp.zeros((N_CMP_PAD, tq), F32)
        for hh in range(NSA_HPG):
            h = g * NSA_HPG + hh
            b = bias_ref[h]
            s = lax.dot_general(kc, q_ref[0, :, h * LANES:(h + 1) * LANES], _NT,
                                preferred_element_type=F32) + b
            e = jnp.exp2(s - jnp.max(s, axis=0, keepdims=True))
            inv = 1.0 / jnp.sum(e, axis=0, keepdims=True)
            p = jnp.where(b > 0.5 * NEG, e * inv, 0.0)
            psum = psum + p
            o = jnp.dot(vct, p.astype(BF16), preferred_element_type=F32)
            lane = GATE_LANE0 + 3 * h
            ot_scr[h * NSA_D:(h + 1) * NSA_D, :] = o * gt_scr[lane:lane + 1, :]
        score = jnp.dot(ovl_ref[...], psum, precision=HIGHEST,
                        preferred_element_type=F32)
        mbt_scr[g * LANES:(g + 1) * LANES, :] = jnp.concatenate(
            [jnp.zeros((MASK_LANE0, tq), F32), _select_mask(score, jj, cur),
             jnp.zeros((LANES - MASK_LANE0 - N_SEL_BLOCKS, tq), F32)], axis=0)
    o_ref[0] = ot_scr[...].T.astype(BF16)
    mb_ref[0] = mbt_scr[...].T.astype(BF16)


def _cmp_attn(z, kc, vct, bias_c, ovl):
    bsz, seq, _ = z.shape
    tq = CHUNK
    return pl.pallas_call(
        _cmp_attn_kernel,
        grid=(bsz, seq // tq),
        in_specs=[pl.BlockSpec((1, tq, Q_WIDTH), lambda b, i: (b, i, ZC_Q // Q_WIDTH)),
                  pl.BlockSpec((1, tq, LANES), lambda b, i: (b, i, ZC_MISC // LANES)),
                  pl.BlockSpec((1, NSA_GROUPS, N_CMP_PAD, LANES), lambda b, i: (b, 0, 0, 0)),
                  pl.BlockSpec((1, NSA_GROUPS, NSA_D, N_CMP_PAD), lambda b, i: (b, 1, 0, 0)),
                  pl.BlockSpec((NSA_HEADS, N_CMP_PAD, tq), lambda b, i: (0, 0, i)),
                  pl.BlockSpec((N_SEL_BLOCKS, N_CMP_PAD), lambda b, i: (0, 0))],
        out_specs=[pl.BlockSpec((1, tq, D_MODEL), lambda b, i: (b, i, 0)),
                   pl.BlockSpec((1, tq, NSA_GROUPS * LANES), lambda b, i: (b, i, 0))],
        out_shape=[jax.ShapeDtypeStruct((bsz, seq, D_MODEL), BF16),
                   jax.ShapeDtypeStruct((bsz, seq, NSA_GROUPS * LANES), BF16)],
        scratch_shapes=[pltpu.VMEM((LANES, tq), F32),
                        pltpu.VMEM((D_MODEL, tq), F32),
                        pltpu.VMEM((NSA_GROUPS * LANES, tq), F32)],
        compiler_params=_cparams(2),
        name="cmp_attn_select",
    )(z, z, kc, vct, bias_c, ovl)


def _flash_produce(s_ref, mx_ref, s):
    s_ref[...] = s
    mx_ref[...] = jnp.max(s, axis=0, keepdims=True)


def _flash_consume(s_ref, mx_ref, vt, m_ref, acc_ref):
    m_old = m_ref[...]
    m_new = jnp.maximum(m_old, mx_ref[...])
    alpha = jnp.exp2(m_old - m_new)
    p = jnp.exp2((s_ref[...] - m_new).astype(BF16))
    acc_ref[...] = alpha * acc_ref[...] + jnp.dot(vt, p, preferred_element_type=F32)
    m_ref[...] = m_new


def _flash_reset(m_ref, acc_ref):
    m_ref[...] = jnp.full(m_ref.shape, NEG, F32)
    acc_ref[...] = jnp.zeros(acc_ref.shape, F32)


def _ones_rows(n_cols):
    row = lax.broadcasted_iota(jnp.int32, (ONES_ROWS, n_cols), 0)
    return jnp.where(row == 0, 1.0, 0.0).astype(BF16)


def _run_flash(lo, hi, produce, consume):
    n = hi - lo + 1
    produce(lo, 0, "first")

    @pl.when(n % 2 == 1)
    def _():
        consume(lo, 0)
        produce(jnp.minimum(lo + 1, hi), 0, "first")

    start = lo + n % 2
    n_pairs = n // 2

    def pair(t, carry):
        c = start + 2 * t
        produce(c + 1, 1, "far")
        consume(c, 0)
        produce(c + 2, 0, "trail")
        consume(c + 1, 1)
        return carry

    lax.fori_loop(0, n_pairs - 1, pair, 0)

    @pl.when(n_pairs >= 1)
    def _():
        produce(hi, 1, "last")
        consume(hi - 1, 0)
        consume(hi, 1)


def _nsa_flash_kernel(*refs, use_mask, n_back, gate_idx):
    if use_mask:
        q_ref, mb_ref, *refs = refs
    else:
        q_ref, *refs = refs
    (misc_ref, k_ref, v_ref, bias_ref, o_ref, kaug, vt, qs_scr, gt_scr, ot_scr,
     s_scr, mx_scr, m_scr, acc_scr) = refs
    i = pl.program_id(1)
    seq = k_ref.shape[1]

    @pl.when(i == 0)
    def _():
        if use_mask:
            blk = lax.broadcasted_iota(jnp.int32, (seq, NSA_D), 0) // SEL_BLOCK
            lane = lax.broadcasted_iota(jnp.int32, (seq, NSA_D), 1)
            extra = jnp.where(blk == lane, 1.0, 0.0).astype(BF16)
        else:
            extra = jnp.zeros((seq, NSA_D), BF16)
        eye = (lax.broadcasted_iota(jnp.int32, (NSA_D, NSA_D), 0)
               == lax.broadcasted_iota(jnp.int32, (NSA_D, NSA_D), 1)).astype(BF16)
        for g in range(NSA_GROUPS):
            gs = slice(g * NSA_D, (g + 1) * NSA_D)
            kaug[g] = jnp.concatenate([k_ref[0, :, gs], extra], axis=1)
            for c in range(seq // CHUNK):
                vt[g, c, 0:NSA_D, :] = lax.dot_general(
                    eye, v_ref[0, c * CHUNK:(c + 1) * CHUNK, gs], _NT,
                    preferred_element_type=F32).astype(BF16)
                vt[g, c, NSA_D:NSA_D + ONES_ROWS, :] = _ones_rows(CHUNK)

    gt_scr[...] = _sigmoid(misc_ref[0].astype(F32)).T
    for h in range(NSA_HEADS):
        g, hh = divmod(h, NSA_HPG)
        qh = q_ref[0, :, h * LANES:(h + 1) * LANES]
        if use_mask:
            qh = qh + mb_ref[0, :, g * LANES:(g + 1) * LANES]
        qs_scr[g, hh * CHUNK:(hh + 1) * CHUNK, :] = qh

    lo = 0 if n_back is None else jnp.maximum(i - n_back, 0)

    groups = range(NSA_GROUPS)
    for g in groups:
        _flash_reset(m_scr.at[g], acc_scr.at[g])

    def produce(c, slot, kind):
        off = pl.multiple_of(c * CHUNK, CHUNK)
        for g in groups:
            s = lax.dot_general(kaug[g, pl.ds(off, CHUNK), :], qs_scr[g], _NT,
                                preferred_element_type=F32)
            if kind == "last":
                s = s + bias_ref[0, g]
            elif kind != "far" or n_back is not None:
                s = s + bias_ref[jnp.minimum(i - c, 2), g]
            _flash_produce(s_scr.at[slot, g], mx_scr.at[slot, g], s)

    def consume(c, slot):
        for g in groups:
            _flash_consume(s_scr.at[slot, g], mx_scr.at[slot, g], vt[g, c],
                           m_scr.at[g], acc_scr.at[g])

    _run_flash(lo, i, produce, consume)
    for h in range(NSA_HEADS):
        g, hh = divmod(h, NSA_HPG)
        lanes = slice(hh * CHUNK, (hh + 1) * CHUNK)
        lane = GATE_LANE0 + 3 * h + gate_idx
        scale = gt_scr[lane:lane + 1, :] / acc_scr[g, NSA_D:NSA_D + 1, lanes]
        ot_scr[h * NSA_D:(h + 1) * NSA_D, :] = acc_scr[g, 0:NSA_D, lanes] * scale
    o_ref[0] = ot_scr[...].T.astype(BF16)


def _nsa_flash(z, mb, bias_t, *, k_col, v_col, n_back, gate_idx):
    bsz, seq, _ = z.shape
    use_mask = mb is not None
    wide = NSA_HPG * CHUNK
    in_specs = [pl.BlockSpec((1, CHUNK, Q_WIDTH), lambda b, i: (b, i, ZC_Q // Q_WIDTH))]
    args = [z]
    if use_mask:
        in_specs.append(pl.BlockSpec((1, CHUNK, NSA_GROUPS * LANES), lambda b, i: (b, i, 0)))
        args.append(mb)
    in_specs += [pl.BlockSpec((1, CHUNK, LANES), lambda b, i: (b, i, ZC_MISC // LANES)),
                 pl.BlockSpec((1, seq, 256), lambda b, i: (b, 0, k_col // 256)),
                 pl.BlockSpec((1, seq, 256), lambda b, i: (b, 0, v_col // 256)),
                 _const_spec((3, NSA_GROUPS, CHUNK, wide))]
    args += [z, z, z, bias_t]
    return pl.pallas_call(
        functools.partial(_nsa_flash_kernel, use_mask=use_mask, n_back=n_back,
                          gate_idx=gate_idx),
        grid=(bsz, seq // CHUNK),
        in_specs=in_specs,
        out_specs=pl.BlockSpec((1, CHUNK, D_MODEL), lambda b, i: (b, i, 0)),
        out_shape=jax.ShapeDtypeStruct((bsz, seq, D_MODEL), BF16),
        scratch_shapes=[pltpu.VMEM((NSA_GROUPS, seq, LANES), BF16),
                        pltpu.VMEM((NSA_GROUPS, seq // CHUNK, NSA_D + ONES_ROWS, CHUNK),
                                   BF16),
                        pltpu.VMEM((NSA_GROUPS, wide, LANES), BF16),
                        pltpu.VMEM((LANES, CHUNK), F32),
                        pltpu.VMEM((D_MODEL, CHUNK), F32),
                        pltpu.VMEM((2, NSA_GROUPS, CHUNK, wide), F32),
                        pltpu.VMEM((2, NSA_GROUPS, 1, wide), F32),
                        pltpu.VMEM((NSA_GROUPS, 1, wide), F32),
                        pltpu.VMEM((NSA_GROUPS, NSA_D + ONES_ROWS, wide), F32)],
        compiler_params=_cparams(2),
        name="slc_attn" if use_mask else "win_attn",
    )(*args)


def _rms(x, gain):
    ms = jnp.mean(x * x, axis=-1, keepdims=True)
    return x * lax.rsqrt(ms + RMS_EPS) * gain


def _mla_prep_kernel(cq_ref, ckv_ref, misc_ref, pos_ref, qg_ref, kvg_ref, wqa_ref, wqb_ref,
                     wk_ref, wvt_ref, pa_ref, pb_ref, frq_ref, q_out, k_out, vt_out):
    tm = cq_ref.shape[1]
    cqn = _rms(cq_ref[0].astype(F32), qg_ref[...]).astype(BF16)
    ckvn = _rms(ckv_ref[0].astype(F32), kvg_ref[...]).astype(BF16)
    ang = pos_ref[0].astype(F32) * frq_ref[...]
    cos = jnp.cos(ang)
    sin = jnp.sin(ang)
    scale = (MLA_NOPE + MLA_ROPE) ** -0.5 * LOG2E
    qa = jnp.dot(cqn, wqa_ref[...], preferred_element_type=F32)
    qb = jnp.dot(cqn, wqb_ref[...], preferred_element_type=F32)
    misc = misc_ref[0]
    kr = (jnp.dot(misc, pa_ref[...], preferred_element_type=F32) * cos
          + jnp.dot(misc, pb_ref[...], preferred_element_type=F32) * sin)
    kn = jnp.dot(ckvn, wk_ref[...], preferred_element_type=F32)
    vt = lax.dot_general(wvt_ref[...], ckvn, _NT,
                         preferred_element_type=F32).astype(BF16)
    for h in range(MLA_HEADS):
        sl = slice(h * LANES, (h + 1) * LANES)
        q_out[0, h] = ((qa[:, sl] * cos + qb[:, sl] * sin) * scale).astype(BF16)
        k_out[0, h] = (kn[:, sl] + kr).astype(BF16)
        for cc in range(tm // CHUNK):
            vt_out[0, h, cc, 0:MLA_V, :] = vt[sl, cc * CHUNK:(cc + 1) * CHUNK]
            vt_out[0, h, cc, MLA_V:MLA_V + ONES_ROWS, :] = _ones_rows(CHUNK)


def _mla_prep(z, pos3, qg, kvg, wqa, wqb, wk, wvt, pa, pb, frq, tm=512):
    bsz, seq, _ = z.shape
    full = lambda shape: pl.BlockSpec(shape, lambda b, i: (0,) * len(shape))
    qk_shape = jax.ShapeDtypeStruct((bsz, MLA_HEADS, seq, LANES), BF16)
    qk_spec = pl.BlockSpec((1, MLA_HEADS, tm, LANES), lambda b, i: (b, 0, i, 0))
    return pl.pallas_call(
        _mla_prep_kernel,
        grid=(bsz, seq // tm),
        in_specs=[pl.BlockSpec((1, tm, MLA_Q_RANK), lambda b, i: (b, i, ZC_CQ // MLA_Q_RANK)),
                  pl.BlockSpec((1, tm, MLA_KV_RANK), lambda b, i: (b, i, ZC_CKV // MLA_KV_RANK)),
                  pl.BlockSpec((1, tm, LANES), lambda b, i: (b, i, ZC_MISC // LANES)),
                  pl.BlockSpec((1, tm, 1), lambda b, i: (b, i, 0)),
                  full((1, MLA_Q_RANK)), full((1, MLA_KV_RANK)),
                  full((MLA_Q_RANK, MLA_HEADS * LANES)), full((MLA_Q_RANK, MLA_HEADS * LANES)),
                  full((MLA_KV_RANK, MLA_HEADS * LANES)), full((MLA_HEADS * MLA_V, MLA_KV_RANK)),
                  full((LANES, LANES)), full((LANES, LANES)), full((1, LANES))],
        out_specs=[qk_spec, qk_spec,
                   pl.BlockSpec((1, MLA_HEADS, tm // CHUNK, MLA_V + ONES_ROWS, CHUNK),
                                lambda b, i: (b, 0, i, 0, 0))],
        out_shape=[qk_shape, qk_shape,
                   jax.ShapeDtypeStruct((bsz, MLA_HEADS, seq // CHUNK, MLA_V + ONES_ROWS, CHUNK),
                                        BF16)],
        compiler_params=_cparams(2),
        name="mla_prep",
    )(z, z, z, pos3, qg, kvg, wqa, wqb, wk, wvt, pa, pb, frq)


def _mla_attn_kernel(q_ref, k_ref, vt_ref, o_ref, ot_scr, s_scr, mx_scr, m_scr, acc_scr):
    i = pl.program_id(1)
    causal = (lax.broadcasted_iota(jnp.int32, (CHUNK, CHUNK), 0)
              <= lax.broadcasted_iota(jnp.int32, (CHUNK, CHUNK), 1))

    heads = range(MLA_HEADS)
    for h in heads:
        _flash_reset(m_scr.at[h], acc_scr.at[h])

    def produce(c, slot, kind):
        off = pl.multiple_of(c * CHUNK, CHUNK)
        for h in heads:
            s = lax.dot_general(k_ref[0, h, pl.ds(off, CHUNK), :], q_ref[0, h], _NT,
                                preferred_element_type=F32)
            if kind == "last":
                s = jnp.where(causal, s, NEG)
            elif kind == "first":
                s = jnp.where(jnp.logical_or(causal, c < i), s, NEG)
            _flash_produce(s_scr.at[slot, h], mx_scr.at[slot, h], s)

    def consume(c, slot):
        for h in heads:
            _flash_consume(s_scr.at[slot, h], mx_scr.at[slot, h], vt_ref[0, h, c],
                           m_scr.at[h], acc_scr.at[h])

    _run_flash(0, i, produce, consume)
    for h in heads:
        ot_scr[h * MLA_V:(h + 1) * MLA_V, :] = (
            acc_scr[h, 0:MLA_V, :] / acc_scr[h, MLA_V:MLA_V + 1, :])
    o_ref[0] = ot_scr[...].T.astype(BF16)


def _mla_attn(q, k, vt):
    bsz, nh, seq, _ = q.shape
    hpt = nh
    return pl.pallas_call(
        _mla_attn_kernel,
        grid=(bsz, seq // CHUNK),
        in_specs=[pl.BlockSpec((1, nh, CHUNK, LANES), lambda b, i: (b, 0, i, 0)),
                  pl.BlockSpec((1, nh, seq, LANES), lambda b, i: (b, 0, 0, 0)),
                  pl.BlockSpec((1, nh, seq // CHUNK, MLA_V + ONES_ROWS, CHUNK),
                               lambda b, i: (b, 0, 0, 0, 0))],
        out_specs=pl.BlockSpec((1, CHUNK, nh * MLA_V), lambda b, i: (b, i, 0)),
        out_shape=jax.ShapeDtypeStruct((bsz, seq, nh * MLA_V), BF16),
        scratch_shapes=[pltpu.VMEM((nh * MLA_V, CHUNK), F32),
                        pltpu.VMEM((2, hpt, CHUNK, CHUNK), F32),
                        pltpu.VMEM((2, hpt, 1, CHUNK), F32),
                        pltpu.VMEM((hpt, 1, CHUNK), F32),
                        pltpu.VMEM((hpt, MLA_V + ONES_ROWS, CHUNK), F32)],
        compiler_params=_cparams(2),
        name="mla_attn",
    )(q, k, vt)


def _merge_kernel(x_ref, mod_ref, oc_ref, os_ref, ow_ref, om_ref, ma_ref, mb_ref, wo_ref,
                  g_ref, x1_ref, h2_ref):
    mod = mod_ref[0]
    o_nsa = oc_ref[0].astype(F32) + os_ref[0].astype(F32) + ow_ref[0].astype(F32)
    y = (_sigmoid(ma_ref[0].astype(F32)) * o_nsa
         + _sigmoid(mb_ref[0].astype(F32)) * om_ref[0].astype(F32))
    a = jnp.dot(y.astype(BF16), wo_ref[...], preferred_element_type=F32)
    x1 = x_ref[0] + mod[:, 2 * D_MODEL:3 * D_MODEL] * a
    x1_ref[0] = x1
    h2 = _modulated_norm(x1, g_ref[...], mod[:, 3 * D_MODEL:4 * D_MODEL],
                         mod[:, 4 * D_MODEL:5 * D_MODEL])
    h2_ref[0] = h2.astype(BF16)


def _merge(x, mod3, o_cmp, o_slc, o_win, o_mla, z, w_o, gain, tm=512):
    bsz, seq, _ = x.shape
    tok = lambda col: pl.BlockSpec((1, tm, D_MODEL), lambda b, i: (b, i, col))
    return pl.pallas_call(
        _merge_kernel,
        grid=(bsz, seq // tm),
        in_specs=[tok(0),
                  pl.BlockSpec((1, 1, 6 * D_MODEL), lambda b, i: (b, 0, 0)),
                  tok(0), tok(0), tok(0), tok(0),
                  tok(ZC_MA // D_MODEL), tok(ZC_MB // D_MODEL),
                  pl.BlockSpec((D_MODEL, D_MODEL), lambda b, i: (0, 0)),
                  pl.BlockSpec((1, D_MODEL), lambda b, i: (0, 0))],
        out_specs=[tok(0), tok(0)],
        out_shape=[jax.ShapeDtypeStruct((bsz, seq, D_MODEL), F32),
                   jax.ShapeDtypeStruct((bsz, seq, D_MODEL), BF16)],
        compiler_params=_cparams(2),
        name="merge_out_proj",
    )(x, mod3, o_cmp, o_slc, o_win, o_mla, z, z, w_o, gain)


def _ffn_kernel(x1_ref, h2_ref, mod_ref, wg_ref, wu_ref, cw_ref, cb_ref, wd_ref, fg_ref,
                o_ref, prev_ref, *, tm, tf):
    i = pl.program_id(1)

    @pl.when(i == 0)
    def _():
        prev_ref[...] = jnp.zeros_like(prev_ref)

    h2 = h2_ref[0]
    row = lax.broadcasted_iota(jnp.int32, (tm, tf), 0)
    acc = jnp.zeros((tm, D_MODEL), F32)
    for f0 in range(0, D_FF, tf):
        fs = slice(f0, f0 + tf)
        gt = jnp.dot(h2, wg_ref[:, fs], preferred_element_type=F32)
        p1 = prev_ref[7:8, fs]
        p2 = prev_ref[6:7, fs]
        g1 = jnp.where(row == 0, p1, pltpu.roll(gt, 1, 0))
        g2 = jnp.where(row == 0, p2, jnp.where(row == 1, p1, pltpu.roll(gt, 2, 0)))
        prev_ref[:, fs] = gt[tm - 8:tm, :]
        conv = cb_ref[:, fs] + cw_ref[0:1, fs] * g2 + cw_ref[1:2, fs] * g1 + cw_ref[2:3, fs] * gt
        up = jnp.dot(h2, wu_ref[:, fs], preferred_element_type=F32)
        act = (conv * _sigmoid(conv)) * up
        acc = acc + jnp.dot(act.astype(BF16), wd_ref[fs, :], preferred_element_type=F32)
    mod = mod_ref[0]
    x2 = x1_ref[0] + mod[:, 5 * D_MODEL:6 * D_MODEL] * acc
    ms = jnp.mean(x2 * x2, axis=-1, keepdims=True)
    o_ref[0] = x2 * lax.rsqrt(ms + RMS_EPS) * fg_ref[...]


def _ffn(x1, h2, mod3, wg, wu, cw, cb, wd, fg, tm=512, tf=1408):
    bsz, seq, _ = x1.shape
    tok = pl.BlockSpec((1, tm, D_MODEL), lambda b, i: (b, i, 0))
    return pl.pallas_call(
        functools.partial(_ffn_kernel, tm=tm, tf=tf),
        grid=(bsz, seq // tm),
        in_specs=[tok, tok,
                  pl.BlockSpec((1, 1, 6 * D_MODEL), lambda b, i: (b, 0, 0)),
                  _const_spec((D_MODEL, D_FF)), _const_spec((D_MODEL, D_FF)),
                  _const_spec((3, D_FF)), _const_spec((1, D_FF)), _const_spec((D_FF, D_MODEL)),
                  _const_spec((1, D_MODEL))],
        out_specs=tok,
        out_shape=jax.ShapeDtypeStruct((bsz, seq, D_MODEL), F32),
        scratch_shapes=[pltpu.VMEM((8, D_FF), F32)],
        compiler_params=_cparams(2),
        name="conv_glu_ffn",
    )(x1, h2, mod3, wg, wu, cw, cb, wd, fg)


def _prep_w_in(w_in):
    off = {}
    o = 0
    for name, width in (("nsa_q", 1024), ("k_cmp", 256), ("v_cmp", 256), ("k_slc", 256),
                        ("v_slc", 256), ("k_win", 256), ("v_win", 256), ("nsa_gate", 48),
                        ("mla_cq", 256), ("mla_ckv", 128), ("mla_krope", 32),
                        ("merge_a", 1024), ("merge_b", 1024)):
        off[name] = (o, o + width)
        o += width
    col = lambda name: w_in[:, off[name][0]:off[name][1]]
    q = (col("nsa_q") * (NSA_D ** -0.5 * LOG2E)).reshape(D_MODEL, NSA_HEADS, NSA_D)
    q_pad = jnp.concatenate([q, jnp.zeros_like(q)], axis=-1).reshape(D_MODEL, Q_WIDTH)
    pad = jnp.zeros((D_MODEL, LANES - MLA_ROPE - 3 * NSA_HEADS), w_in.dtype)
    parts = [q_pad, col("merge_a"), col("merge_b"), col("k_slc"), col("v_slc"), col("k_win"),
             col("v_win"), col("mla_cq"), col("mla_ckv"), col("mla_krope"), col("nsa_gate"),
             pad, col("k_cmp"), col("v_cmp")]
    return jnp.concatenate(parts, axis=1).astype(BF16)


def _prep_mla_weights(w_uq, w_ukv):
    half = MLA_ROPE // 2
    dq = MLA_NOPE + MLA_ROPE
    wq = w_uq.reshape(MLA_Q_RANK, MLA_HEADS, dq)
    nope, x1, x2 = wq[..., :MLA_NOPE], wq[..., MLA_NOPE:MLA_NOPE + half], wq[..., MLA_NOPE + half:]
    z32 = jnp.zeros((MLA_Q_RANK, MLA_HEADS, LANES - dq), w_uq.dtype)
    wqa = jnp.concatenate([nope, x1, x2, z32], axis=-1)
    wqb = jnp.concatenate([jnp.zeros_like(nope), -x2, x1, z32], axis=-1)
    wkv = w_ukv.reshape(MLA_KV_RANK, MLA_HEADS, MLA_NOPE + MLA_V)
    wk = jnp.concatenate([wkv[..., :MLA_NOPE],
                          jnp.zeros((MLA_KV_RANK, MLA_HEADS, LANES - MLA_NOPE), w_ukv.dtype)],
                         axis=-1)
    wv = wkv[..., MLA_NOPE:]
    flat = lambda w: w.reshape(w.shape[0], MLA_HEADS * LANES).astype(BF16)
    return flat(wqa), flat(wqb), flat(wk), flat(wv).T


def _rope_constants():
    half = MLA_ROPE // 2
    pa = np.zeros((LANES, LANES), np.float32)
    pb = np.zeros((LANES, LANES), np.float32)
    for j in range(half):
        pa[j, MLA_NOPE + j] = 1.0
        pa[half + j, MLA_NOPE + half + j] = 1.0
        pb[half + j, MLA_NOPE + j] = -1.0
        pb[j, MLA_NOPE + half + j] = 1.0
    inv_freq = ROPE_THETA ** (-jnp.arange(0, MLA_ROPE, 2, dtype=F32) / MLA_ROPE)
    frq = jnp.zeros((1, LANES), F32)
    frq = frq.at[0, MLA_NOPE:MLA_NOPE + half].set(inv_freq)
    frq = frq.at[0, MLA_NOPE + half:MLA_NOPE + 2 * half].set(inv_freq)
    return jnp.asarray(pa, BF16), jnp.asarray(pb, BF16), frq


def _overlap(seq):
    nc = (seq - CMP_LEN) // CMP_STRIDE + 1
    nb = seq // SEL_BLOCK
    cs = np.arange(nc) * CMP_STRIDE
    bs = np.arange(nb) * SEL_BLOCK
    ov = np.clip(np.minimum(cs[:, None] + CMP_LEN, bs[None, :] + SEL_BLOCK)
                 - np.maximum(cs[:, None], bs[None, :]), 0, None) / CMP_LEN
    out = np.zeros((nb, N_CMP_PAD), np.float32)
    out[:, :nc] = ov.T
    return jnp.asarray(out)


def kernel(x, c, positions, rel_bias_table, ada_w, ada_b, norm_mix_g, w_in, cmp_pos_k, cmp_w1_k, cmp_w2_k, cmp_pos_v, cmp_w1_v, cmp_w2_v, mla_q_norm_g, mla_w_uq, mla_kv_norm_g, mla_w_ukv, w_o, norm_ffn_g, ffn_w_gate, ffn_w_up, ffn_conv_w, ffn_conv_b, ffn_w_down, final_norm_g):
    bsz, seq, _ = x.shape
    assert ada_w.shape[0] == 1 and seq == N_SEL_BLOCKS * SEL_BLOCK
    n_back = WINDOW // CHUNK

    bias_c = _cmp_bias_table(rel_bias_table, seq)
    bias_s = _chunk_bias_table(rel_bias_table, limit=None, sub_far=True)
    bias_w = _chunk_bias_table(rel_bias_table, limit=WINDOW, sub_far=False)

    mod3 = _ada(c, ada_w[0], ada_b[0]).reshape(bsz, 1, 6 * D_MODEL)
    z, cmp_in = _inproj(x, mod3, norm_mix_g, _prep_w_in(w_in[0]))

    kvc, kvct = _compress(cmp_in,
                          jnp.stack([cmp_pos_k[0].reshape(1, -1), cmp_pos_v[0].reshape(1, -1)]),
                          jnp.stack([cmp_w1_k[0], cmp_w1_v[0]]),
                          jnp.stack([cmp_w2_k[0], cmp_w2_v[0]]))

    o_cmp, mb = _cmp_attn(z, kvc, kvct, bias_c, _overlap(seq))
    o_slc = _nsa_flash(z, mb, bias_s, k_col=ZC_KS, v_col=ZC_VS, n_back=None, gate_idx=1)
    o_win = _nsa_flash(z, None, bias_w, k_col=ZC_KW, v_col=ZC_VW, n_back=n_back, gate_idx=2)

    wqa, wqb, wk, wvt = _prep_mla_weights(mla_w_uq[0], mla_w_ukv[0])
    pa, pb, frq = _rope_constants()
    q_m, k_m, vt_m = _mla_prep(z, positions.reshape(bsz, seq, 1), mla_q_norm_g, mla_kv_norm_g,
                               wqa, wqb, wk, wvt, pa, pb, frq)
    o_mla = _mla_attn(q_m, k_m, vt_m)

    x1, h2 = _merge(x, mod3, o_cmp, o_slc, o_win, o_mla, z, w_o[0].astype(BF16), norm_ffn_g)
    return _ffn(x1, h2, mod3, ffn_w_gate[0].astype(BF16), ffn_w_up[0].astype(BF16),
                ffn_conv_w[0], ffn_conv_b, ffn_w_down[0].astype(BF16),
                final_norm_g.reshape(1, D_MODEL))
```

```python
import functools
import math

import numpy as np
import jax
import jax.numpy as jnp
from jax import lax
from jax.experimental import pallas as pl
from jax.experimental.pallas import tpu as pltpu

F32 = jnp.float32
BF16 = jnp.bfloat16
HIGHEST = lax.Precision.HIGHEST

D_MODEL = 1024
NSA_HEADS = 16
NSA_GROUPS = 4
NSA_HPG = 4
NSA_D = 64
CMP_LEN = 32
CMP_STRIDE = 16
CMP_HIDDEN = 256
SEL_BLOCK = 64
SEL_TOPK = 16
WINDOW = 512
MLA_HEADS = 8
MLA_Q_RANK = 256
MLA_KV_RANK = 128
MLA_NOPE = 64
MLA_ROPE = 32
MLA_V = 128
ROPE_THETA = 10000.0
REL_BUCKETS = 32
REL_MAX_DIST = 128
D_FF = 2816
RMS_EPS = 1e-6
NEG = -1e30
LOG2E = math.log2(math.e)

LANES = 128
CHUNK = 256
N_CMP_PAD = 128
N_SEL_BLOCKS = 32
ONES_ROWS = 16
FFN_COL_SPLITS = (0, 1536, D_FF)

ZC_MA = 0
ZC_MB = 1024
ZC_KS = 2048
ZC_VS = 2304
ZC_KW = 2560
ZC_VW = 2816
ZC_CQ = 3072
ZC_CKV = 3328
ZC_MISC = 3456
Z_WIDTH = 3584
CMP_WIDTH = 512
QT_ROWS = NSA_HEADS * LANES
GATE_LANE0 = MLA_ROPE
MASK_ROW0 = NSA_D

VMEM_LIMIT = 56 * 1024 * 1024

_NT = (((1,), (1,)), ((), ()))


def _cparams(n_axes):
    return pltpu.CompilerParams(
        dimension_semantics=("arbitrary",) * n_axes,
        vmem_limit_bytes=VMEM_LIMIT)


def _sigmoid(x):
    return 1.0 / (1.0 + jnp.exp(-x))


def _const_spec(shape):
    return pl.BlockSpec(shape, lambda *_: (0,) * len(shape), pipeline_mode=pl.Buffered(1))


def _t5_bucket(dist):
    n = jnp.maximum(dist, 0)
    exact = REL_BUCKETS // 2
    nf = jnp.maximum(n, exact).astype(F32)
    log_ratio = jnp.log(nf / exact) / math.log(REL_MAX_DIST / exact)
    large = jnp.minimum(exact + (log_ratio * (REL_BUCKETS - exact)).astype(jnp.int32),
                        REL_BUCKETS - 1)
    return jnp.where(n < exact, n, large)


def _bias_kernel(rel_ref, out_ref, *, rows, cols, row_coef, offset, limit, sub_far,
                 group_lanes, n_real):
    m = pl.program_id(0)
    r = lax.broadcasted_iota(jnp.int32, (rows, cols), 0)
    c = lax.broadcasted_iota(jnp.int32, (rows, cols), 1)
    dist = m * cols + c + row_coef * r + offset
    valid = (dist >= 0) & (m < n_real)
    if limit is not None:
        valid = valid & (dist < limit)
    bucket = _t5_bucket(dist)
    for h in range(NSA_HEADS):
        val = jnp.zeros((rows, cols), F32)
        for b in range(REL_BUCKETS):
            val = jnp.where(bucket == b, rel_ref[b, h], val)
        if sub_far:
            val = val - rel_ref[REL_BUCKETS - 1, h]
        val = jnp.where(valid, val * LOG2E, NEG)
        if group_lanes:
            hh = h % NSA_HPG
            out_ref[0, h // NSA_HPG, :, hh * cols:(hh + 1) * cols] = val
        else:
            out_ref[h] = val


def _cmp_bias_table(rel, seq):
    return pl.pallas_call(
        functools.partial(_bias_kernel, rows=N_CMP_PAD, cols=CHUNK, row_coef=-CMP_STRIDE,
                          offset=-(CMP_LEN - 1), limit=None, sub_far=False, group_lanes=False,
                          n_real=seq // CHUNK),
        grid=(seq // CHUNK,),
        in_specs=[pl.BlockSpec(memory_space=pltpu.SMEM)],
        out_specs=pl.BlockSpec((NSA_HEADS, N_CMP_PAD, CHUNK), lambda m: (0, 0, m)),
        out_shape=jax.ShapeDtypeStruct((NSA_HEADS, N_CMP_PAD, seq), F32),
        compiler_params=_cparams(1),
        name="bias_cmp",
    )(rel)


def _chunk_bias_table(rel, *, limit, sub_far, n_tiles):
    return pl.pallas_call(
        functools.partial(_bias_kernel, rows=CHUNK, cols=CHUNK, row_coef=-1, offset=0,
                          limit=limit, sub_far=sub_far, group_lanes=True, n_real=3),
        grid=(n_tiles,),
        in_specs=[pl.BlockSpec(memory_space=pltpu.SMEM)],
        out_specs=pl.BlockSpec((1, NSA_GROUPS, CHUNK, NSA_HPG * CHUNK), lambda m: (m, 0, 0, 0)),
        out_shape=jax.ShapeDtypeStruct((n_tiles, NSA_GROUPS, CHUNK, NSA_HPG * CHUNK), F32),
        compiler_params=_cparams(1),
        name="bias_chunk",
    )(rel)


def _ada_kernel(c_ref, w_ref, b_ref, o_ref):
    c = c_ref[...]
    cond = c * _sigmoid(c)
    o_ref[...] = jnp.dot(cond, w_ref[...], precision=HIGHEST,
                         preferred_element_type=F32) + b_ref[...]


def _ada(c, w, b):
    bsz = c.shape[0]
    n = w.shape[1]
    tn = 1024
    return pl.pallas_call(
        _ada_kernel,
        grid=(n // tn,),
        in_specs=[pl.BlockSpec((bsz, D_MODEL), lambda j: (0, 0)),
                  pl.BlockSpec((D_MODEL, tn), lambda j: (0, j)),
                  pl.BlockSpec((1, tn), lambda j: (0, j))],
        out_specs=pl.BlockSpec((bsz, tn), lambda j: (0, j)),
        out_shape=jax.ShapeDtypeStruct((bsz, n), F32),
        compiler_params=_cparams(1),
        name="ada_mod",
    )(c, w, b.reshape(1, n))


def _modulated_norm(x, gain, shift, scale):
    ms = jnp.mean(x * x, axis=-1, keepdims=True)
    return (x * lax.rsqrt(ms + RMS_EPS) * gain) * (1.0 + scale) + shift


def _inproj_kernel(x_ref, mod_ref, g_ref, w_ref, wq_ref, z_ref, qt_ref, cmp_ref, *, tn):
    mod = mod_ref[0]
    h = _modulated_norm(x_ref[0], g_ref[...], mod[:, 0:D_MODEL],
                        mod[:, D_MODEL:2 * D_MODEL])
    hb = h.astype(BF16)
    tm = hb.shape[0]
    qt = lax.dot_general(wq_ref[...], hb, _NT, preferred_element_type=F32)
    zeros = jnp.zeros((LANES - NSA_D, tm), BF16)
    for hd in range(NSA_HEADS):
        qt_ref[0, hd * LANES:hd * LANES + NSA_D, :] = (
            qt[hd * NSA_D:(hd + 1) * NSA_D, :].astype(BF16))
        qt_ref[0, hd * LANES + NSA_D:(hd + 1) * LANES, :] = zeros
    for n0 in range(0, Z_WIDTH, tn):
        z_ref[0, :, n0:n0 + tn] = jnp.dot(
            hb, w_ref[:, n0:n0 + tn], preferred_element_type=F32).astype(BF16)
    acc = jnp.dot(hb, w_ref[:, Z_WIDTH:Z_WIDTH + CMP_WIDTH], preferred_element_type=F32)
    for g in range(NSA_GROUPS):
        cmp_ref[0, g] = acc[:, g * LANES:(g + 1) * LANES]


def _inproj(x, mod3, gain, w_p, wq_t, tm=512):
    bsz, seq, _ = x.shape
    return pl.pallas_call(
        functools.partial(_inproj_kernel, tn=512),
        grid=(bsz, seq // tm),
        in_specs=[pl.BlockSpec((1, tm, D_MODEL), lambda b, i: (b, i, 0)),
                  pl.BlockSpec((1, 1, 6 * D_MODEL), lambda b, i: (b, 0, 0)),
                  pl.BlockSpec((1, D_MODEL), lambda b, i: (0, 0)),
                  _const_spec(w_p.shape), _const_spec(wq_t.shape)],
        out_specs=[pl.BlockSpec((1, tm, Z_WIDTH), lambda b, i: (b, i, 0)),
                   pl.BlockSpec((1, QT_ROWS, tm), lambda b, i: (b, 0, i)),
                   pl.BlockSpec((1, NSA_GROUPS, tm, LANES), lambda b, i: (b, 0, i, 0))],
        out_shape=[jax.ShapeDtypeStruct((bsz, seq, Z_WIDTH), BF16),
                   jax.ShapeDtypeStruct((bsz, QT_ROWS, seq), BF16),
                   jax.ShapeDtypeStruct((bsz, NSA_GROUPS, seq, LANES), F32)],
        compiler_params=_cparams(2),
        name="in_proj",
    )(x, mod3, gain, w_p, wq_t)


def _gelu_tanh(x):
    return 0.5 * x * (1.0 + jnp.tanh(math.sqrt(2.0 / math.pi) * (x + 0.044715 * (x * x * x))))


def _pos_term_kernel(pos_ref, w1_ref, o_ref):
    pos = jnp.broadcast_to(pos_ref[0], (8, CMP_LEN * NSA_D))
    o_ref[0] = jnp.dot(pos, w1_ref[0], precision=HIGHEST, preferred_element_type=F32)


def _pos_term(pos, w1):
    return pl.pallas_call(
        _pos_term_kernel,
        grid=(2,),
        in_specs=[pl.BlockSpec((1, 1, CMP_LEN * NSA_D), lambda j: (j, 0, 0)),
                  pl.BlockSpec((1, CMP_LEN * NSA_D, CMP_HIDDEN), lambda j: (j, 0, 0))],
        out_specs=pl.BlockSpec((1, 8, CMP_HIDDEN), lambda j: (j, 0, 0)),
        out_shape=jax.ShapeDtypeStruct((2, 8, CMP_HIDDEN), F32),
        compiler_params=_cparams(1),
        name="cmp_pos_term",
    )(pos, w1)


def _compress_kernel(y_ref, pterm_ref, w1_ref, w2_ref, w2t_ref, kc_ref, vct_ref):
    nhb = y_ref.shape[2] // CMP_STRIDE
    low = lax.broadcasted_iota(jnp.int32, (nhb, LANES), 1) < NSA_D
    k_cols, v_cols = [], []
    for r in range(0, CMP_STRIDE, 2):
        a = y_ref[0, 0, pl.ds(r, nhb, stride=CMP_STRIDE), :]
        b = y_ref[0, 0, pl.ds(r + 1, nhb, stride=CMP_STRIDE), :]
        k_cols.append(jnp.where(low, a, pltpu.roll(b, NSA_D, 1)))
        v_cols.append(jnp.where(low, pltpu.roll(a, NSA_D, 1), b))
    half = (CMP_LEN // 2) * NSA_D
    outs = []
    for kv, cols in enumerate((k_cols, v_cols)):
        y = jnp.concatenate(cols, axis=1).astype(BF16)
        top = jnp.dot(y, w1_ref[kv, 0:half, :], preferred_element_type=F32)
        bot = jnp.dot(y, w1_ref[kv, half:2 * half, :], preferred_element_type=F32)
        hidden = top + pltpu.roll(bot, N_CMP_PAD - 1, 0) + pterm_ref[kv, 0:1, :]
        outs.append(_gelu_tanh(hidden).astype(BF16))
    out = jnp.dot(outs[0], w2_ref[0], preferred_element_type=F32).astype(BF16)
    kc_ref[0, 0] = jnp.concatenate([out, jnp.zeros_like(out)], axis=1)
    vct_ref[0, 0] = lax.dot_general(w2t_ref[1], outs[1], _NT,
                                    preferred_element_type=F32).astype(BF16)


def _compress(cmp_in, pos, w1, w2):
    bsz, ng, seq, _ = cmp_in.shape
    nhb = seq // CMP_STRIDE
    w2b = w2.astype(BF16)
    return pl.pallas_call(
        _compress_kernel,
        grid=(bsz, ng),
        in_specs=[pl.BlockSpec((1, 1, seq, LANES), lambda b, g: (b, g, 0, 0)),
                  _const_spec((2, 8, CMP_HIDDEN)),
                  _const_spec((2, CMP_LEN * NSA_D, CMP_HIDDEN)),
                  _const_spec((2, CMP_HIDDEN, NSA_D)),
                  _const_spec((2, NSA_D, CMP_HIDDEN))],
        out_specs=[pl.BlockSpec((1, 1, nhb, LANES), lambda b, g: (b, g, 0, 0)),
                   pl.BlockSpec((1, 1, NSA_D, nhb), lambda b, g: (b, g, 0, 0))],
        out_shape=[jax.ShapeDtypeStruct((bsz, ng, nhb, LANES), BF16),
                   jax.ShapeDtypeStruct((bsz, ng, NSA_D, nhb), BF16)],
        compiler_params=_cparams(2),
        name="compress",
    )(cmp_in, _pos_term(pos, w1), w1.astype(BF16), w2b, w2b.transpose(0, 2, 1))


def _select_mask(score, jj, cur):
    forced = (jj == 0) | (jj == cur) | (jj == cur - 1)
    x = jnp.where(forced, jnp.inf, jnp.where(jj > cur, -jnp.inf, score))
    below = pltpu.bitcast(pltpu.bitcast(x, jnp.int32) - 1, F32)
    x_lo = jnp.where(x > 0.0, below, jnp.where(x == 0.0, -1.0, x))
    tq = x.shape[1]
    n_sub = 8
    masks = []
    for v in range(N_SEL_BLOCKS // n_sub):
        rows = slice(v * n_sub, (v + 1) * n_sub)
        xv, xlv = x[rows], x_lo[rows]
        jv = v * n_sub + lax.broadcasted_iota(jnp.int32, (n_sub, tq), 0)
        rank = jnp.zeros((n_sub, tq), F32)
        for j2 in range(N_SEL_BLOCKS):
            row = x[j2:j2 + 1, :]
            if v * n_sub > j2:
                thr = xlv
            elif (v + 1) * n_sub - 1 <= j2:
                thr = xv
            else:
                thr = jnp.where(jv > j2, xlv, xv)
            rank = rank + jnp.where(row > thr, 1.0, 0.0)
        masks.append(jnp.where(rank < SEL_TOPK, 0.0, NEG))
    return jnp.concatenate(masks, axis=0)


def _cmp_attn_kernel(q_ref, misc_ref, kc_ref, vct_ref, bias_ref, ovl_ref, o_ref, mb_ref,
                     gt_scr, ot_scr):
    i = pl.program_id(1)
    tq = CHUNK
    gt_scr[...] = _sigmoid(misc_ref[0].astype(F32)).T
    jj = lax.broadcasted_iota(jnp.int32, (N_SEL_BLOCKS, tq), 0)
    tt = i * tq + lax.broadcasted_iota(jnp.int32, (N_SEL_BLOCKS, tq), 1)
    cur = tt // SEL_BLOCK
    t_row = i * tq + lax.broadcasted_iota(jnp.int32, (1, tq), 1)
    sees_any = t_row >= CMP_LEN - 1
    logits = [jnp.dot(kc_ref[0, h // NSA_HPG], q_ref[0, h * LANES:(h + 1) * LANES, :],
                      preferred_element_type=F32) + bias_ref[h]
              for h in range(NSA_HEADS)]
    for g in range(NSA_GROUPS):
        vct = vct_ref[0, g]
        psum = jnp.zeros((N_CMP_PAD, tq), F32)
        for hh in range(NSA_HPG):
            h = g * NSA_HPG + hh
            s = logits[h]
            e = jnp.exp2(s - jnp.max(s, axis=0, keepdims=True))
            inv = jnp.where(sees_any, 1.0 / jnp.sum(e, axis=0, keepdims=True), 0.0)
            p = e * inv
            psum = psum + p
            o = jnp.dot(vct, p.astype(BF16), preferred_element_type=F32)
            lane = GATE_LANE0 + 3 * h
            ot_scr[h * NSA_D:(h + 1) * NSA_D, :] = o * gt_scr[lane:lane + 1, :]
        score = jnp.dot(ovl_ref[...], psum, precision=HIGHEST,
                        preferred_element_type=F32)
        mb_ref[0, g * LANES:(g + 1) * LANES, :] = jnp.concatenate(
            [jnp.zeros((MASK_ROW0, tq), F32), _select_mask(score, jj, cur),
             jnp.zeros((LANES - MASK_ROW0 - N_SEL_BLOCKS, tq), F32)], axis=0).astype(BF16)
    o_ref[0] = ot_scr[...].T.astype(BF16)


def _cmp_attn(z, qt, kc, vct, bias_c, ovl):
    bsz, seq, _ = z.shape
    tq = CHUNK
    return pl.pallas_call(
        _cmp_attn_kernel,
        grid=(bsz, seq // tq),
        in_specs=[pl.BlockSpec((1, QT_ROWS, tq), lambda b, i: (b, 0, i)),
                  pl.BlockSpec((1, tq, LANES), lambda b, i: (b, i, ZC_MISC // LANES)),
                  pl.BlockSpec((1, NSA_GROUPS, N_CMP_PAD, LANES), lambda b, i: (b, 0, 0, 0)),
                  pl.BlockSpec((1, NSA_GROUPS, NSA_D, N_CMP_PAD), lambda b, i: (b, 0, 0, 0)),
                  pl.BlockSpec((NSA_HEADS, N_CMP_PAD, tq), lambda b, i: (0, 0, i)),
                  pl.BlockSpec((N_SEL_BLOCKS, N_CMP_PAD), lambda b, i: (0, 0))],
        out_specs=[pl.BlockSpec((1, tq, D_MODEL), lambda b, i: (b, i, 0)),
                   pl.BlockSpec((1, NSA_GROUPS * LANES, tq), lambda b, i: (b, 0, i))],
        out_shape=[jax.ShapeDtypeStruct((bsz, seq, D_MODEL), BF16),
                   jax.ShapeDtypeStruct((bsz, NSA_GROUPS * LANES, seq), BF16)],
        scratch_shapes=[pltpu.VMEM((LANES, tq), F32),
                        pltpu.VMEM((D_MODEL, tq), F32)],
        compiler_params=_cparams(2),
        name="cmp_attn_select",
    )(qt, z, kc, vct, bias_c, ovl)


def _flash_produce(s_ref, mx_ref, s):
    s_ref[...] = s
    mx_ref[...] = jnp.max(s, axis=0, keepdims=True)


def _flash_consume(s_ref, mx_ref, vt, m_ref, acc_ref):
    m_old = m_ref[...]
    m_new = jnp.maximum(m_old, mx_ref[...])
    alpha = jnp.exp2(m_old - m_new)
    p = jnp.exp2((s_ref[...] - m_new).astype(BF16))
    acc_ref[...] = alpha * acc_ref[...] + jnp.dot(vt, p, preferred_element_type=F32)
    m_ref[...] = m_new


def _flash_reset(m_ref, acc_ref):
    m_ref[...] = jnp.full(m_ref.shape, NEG, F32)
    acc_ref[...] = jnp.zeros(acc_ref.shape, F32)


def _ones_rows(n_cols):
    row = lax.broadcasted_iota(jnp.int32, (ONES_ROWS, n_cols), 0)
    return jnp.where(row == 0, 1.0, 0.0).astype(BF16)


def _run_flash(lo, hi, streams, produce_one, consume_one, common_n=None):
    n = hi - lo + 1

    def produce(c, slot, kind):
        for s in streams:
            produce_one(s, c, slot, kind)

    def consume(c, slot):
        for s in streams:
            consume_one(s, c, slot)

    def both(cp, sp, kind, cc, sc):
        for s in streams:
            produce_one(s, cp, sp, kind)
            consume_one(s, cc, sc)

    def generic():
        produce(lo, 0, "first")

        @pl.when(n % 2 == 1)
        def _():
            nxt = jnp.minimum(lo + 1, hi)
            for s in streams:
                consume_one(s, lo, 0)
                produce_one(s, nxt, 0, "first")

        start = lo + n % 2
        n_pairs = n // 2

        def pair(t, carry):
            c = start + 2 * t
            both(c + 1, 1, "far", c, 0)
            both(c + 2, 0, "trail", c + 1, 1)
            return carry

        lax.fori_loop(0, n_pairs - 1, pair, 0)

        @pl.when(n_pairs >= 1)
        def _():
            both(hi, 1, "last", hi - 1, 0)
            consume(hi, 1)

    if common_n is None:
        generic()
        return

    @pl.when(n == common_n)
    def _():
        produce(lo, 0, "first")
        for k in range(common_n - 1):
            both(lo + k + 1, (k + 1) % 2, "last" if k + 2 == common_n else "first",
                 lo + k, k % 2)
        consume(hi, (common_n - 1) % 2)

    @pl.when(n != common_n)
    def _():
        generic()


def _nsa_flash_kernel(*refs, use_mask, n_back, gate_idx):
    if use_mask:
        q_ref, mb_ref, *refs = refs
    else:
        q_ref, *refs = refs
    misc_ref, k_ref, v_ref, bias_ref, o_ref, kaug, vt, qs_scr, gt_scr, ot_scr, *flash_scr = refs
    i = pl.program_id(1)
    seq = k_ref.shape[1]

    @pl.when(i == 0)
    def _():
        if use_mask:
            blk = lax.broadcasted_iota(jnp.int32, (seq, NSA_D), 0) // SEL_BLOCK
            lane = lax.broadcasted_iota(jnp.int32, (seq, NSA_D), 1)
            extra = jnp.where(blk == lane, 1.0, 0.0).astype(BF16)
        else:
            extra = jnp.zeros((seq, NSA_D), BF16)
        eye = (lax.broadcasted_iota(jnp.int32, (NSA_D, NSA_D), 0)
               == lax.broadcasted_iota(jnp.int32, (NSA_D, NSA_D), 1)).astype(BF16)
        for g in range(NSA_GROUPS):
            gs = slice(g * NSA_D, (g + 1) * NSA_D)
            kaug[g] = jnp.concatenate([k_ref[0, :, gs], extra], axis=1)
            for c in range(seq // CHUNK):
                vt[g, c, 0:NSA_D, :] = lax.dot_general(
                    eye, v_ref[0, c * CHUNK:(c + 1) * CHUNK, gs], _NT,
                    preferred_element_type=F32).astype(BF16)
                vt[g, c, NSA_D:NSA_D + ONES_ROWS, :] = _ones_rows(CHUNK)

    gt_scr[...] = _sigmoid(misc_ref[0].astype(F32)).T
    for h in range(NSA_HEADS):
        g, hh = divmod(h, NSA_HPG)
        qh = q_ref[0, h * LANES:(h + 1) * LANES, :]
        if use_mask:
            qh = qh + mb_ref[0, g * LANES:(g + 1) * LANES, :]
        qs_scr[g, :, hh * CHUNK:(hh + 1) * CHUNK] = qh

    groups = range(NSA_GROUPS)

    def logits(g, c):
        off = pl.multiple_of(c * CHUNK, CHUNK)
        return jnp.dot(kaug[g, pl.ds(off, CHUNK), :], qs_scr[g],
                       preferred_element_type=F32)

    def store_group(g, acc):
        for hh in range(NSA_HPG):
            h = g * NSA_HPG + hh
            lanes = slice(hh * CHUNK, (hh + 1) * CHUNK)
            lane = GATE_LANE0 + 3 * h + gate_idx
            scale = gt_scr[lane:lane + 1, :] / acc[NSA_D:NSA_D + 1, lanes]
            ot_scr[h * NSA_D:(h + 1) * NSA_D, :] = acc[0:NSA_D, lanes] * scale

    s_scr, mx_scr, m_scr, acc_scr = flash_scr
    for g in groups:
        _flash_reset(m_scr.at[g], acc_scr.at[g])

    def produce(g, c, slot, kind):
        s = logits(g, c)
        if kind == "last":
            s = s + bias_ref[0, g]
        elif kind != "far" or n_back is not None:
            s = s + bias_ref[jnp.minimum(i - c, 2), g]
        _flash_produce(s_scr.at[slot, g], mx_scr.at[slot, g], s)

    def consume(g, c, slot):
        _flash_consume(s_scr.at[slot, g], mx_scr.at[slot, g], vt[g, c],
                       m_scr.at[g], acc_scr.at[g])

    if n_back is None:
        _run_flash(0, i, groups, produce, consume)
    else:
        _run_flash(jnp.maximum(i - n_back, 0), i, groups, produce, consume,
                   common_n=n_back + 1)
    for g in groups:
        store_group(g, acc_scr[g])
    o_ref[0] = ot_scr[...].T.astype(BF16)


def _nsa_flash(z, qt, mb, bias_t, *, k_col, v_col, n_back, gate_idx):
    bsz, seq, _ = z.shape
    use_mask = mb is not None
    wide = NSA_HPG * CHUNK
    in_specs = [pl.BlockSpec((1, QT_ROWS, CHUNK), lambda b, i: (b, 0, i))]
    args = [qt]
    if use_mask:
        in_specs.append(pl.BlockSpec((1, NSA_GROUPS * LANES, CHUNK), lambda b, i: (b, 0, i)))
        args.append(mb)
    in_specs += [pl.BlockSpec((1, CHUNK, LANES), lambda b, i: (b, i, ZC_MISC // LANES)),
                 pl.BlockSpec((1, seq, 256), lambda b, i: (b, 0, k_col // 256)),
                 pl.BlockSpec((1, seq, 256), lambda b, i: (b, 0, v_col // 256)),
                 _const_spec(bias_t.shape)]
    args += [z, z, z, bias_t]
    scratch = [pltpu.VMEM((NSA_GROUPS, seq, LANES), BF16),
               pltpu.VMEM((NSA_GROUPS, seq // CHUNK, NSA_D + ONES_ROWS, CHUNK), BF16),
               pltpu.VMEM((NSA_GROUPS, LANES, wide), BF16),
               pltpu.VMEM((LANES, CHUNK), F32),
               pltpu.VMEM((D_MODEL, CHUNK), F32),
               pltpu.VMEM((2, NSA_GROUPS, CHUNK, wide), F32),
               pltpu.VMEM((2, NSA_GROUPS, 1, wide), F32),
               pltpu.VMEM((NSA_GROUPS, 1, wide), F32),
               pltpu.VMEM((NSA_GROUPS, NSA_D + ONES_ROWS, wide), F32)]
    return pl.pallas_call(
        functools.partial(_nsa_flash_kernel, use_mask=use_mask, n_back=n_back,
                          gate_idx=gate_idx),
        grid=(bsz, seq // CHUNK),
        in_specs=in_specs,
        out_specs=pl.BlockSpec((1, CHUNK, D_MODEL), lambda b, i: (b, i, 0)),
        out_shape=jax.ShapeDtypeStruct((bsz, seq, D_MODEL), BF16),
        scratch_shapes=scratch,
        compiler_params=_cparams(2),
        name="slc_attn" if use_mask else "win_attn",
    )(*args)


def _rms(x, gain):
    ms = jnp.mean(x * x, axis=-1, keepdims=True)
    return x * lax.rsqrt(ms + RMS_EPS) * gain


def _mla_prep_kernel(cq_ref, ckv_ref, misc_ref, pos_ref, qg_ref, kvg_ref, wqa_ref, wqb_ref,
                     wk_ref, wvt_ref, pa_ref, pb_ref, frq_ref, q_out, k_out, vt_out):
    tm = cq_ref.shape[1]
    cqn = _rms(cq_ref[0].astype(F32), qg_ref[...]).astype(BF16)
    ckvn = _rms(ckv_ref[0].astype(F32), kvg_ref[...]).astype(BF16)
    half = MLA_ROPE // 2
    ang = frq_ref[...] * pos_ref[0].astype(F32)
    cos_t, sin_t = jnp.cos(ang), jnp.sin(ang)
    ones = jnp.ones((MLA_NOPE, tm), F32)
    zeros = jnp.zeros((LANES - MLA_NOPE - 2 * half, tm), F32)
    cos_rows = jnp.concatenate([ones, cos_t, cos_t, zeros], axis=0)
    sin_rows = jnp.concatenate([0.0 * ones, sin_t, sin_t, zeros], axis=0)
    cos, sin = cos_rows.T, sin_rows.T
    scale = (MLA_NOPE + MLA_ROPE) ** -0.5 * LOG2E
    qa = lax.dot_general(wqa_ref[...], cqn, _NT, preferred_element_type=F32)
    qb = lax.dot_general(wqb_ref[...], cqn, _NT, preferred_element_type=F32)
    misc = misc_ref[0]
    kr = (jnp.dot(misc, pa_ref[...], preferred_element_type=F32) * cos
          + jnp.dot(misc, pb_ref[...], preferred_element_type=F32) * sin)
    kn = jnp.dot(ckvn, wk_ref[...], preferred_element_type=F32)
    vt = lax.dot_general(wvt_ref[...], ckvn, _NT,
                         preferred_element_type=F32).astype(BF16)
    for h in range(MLA_HEADS):
        sl = slice(h * LANES, (h + 1) * LANES)
        q_out[0, h] = ((qa[sl, :] * cos_rows + qb[sl, :] * sin_rows) * scale).astype(BF16)
        k_out[0, h] = (kn[:, sl] + kr).astype(BF16)
        for cc in range(tm // CHUNK):
            vt_out[0, h, cc, 0:MLA_V, :] = vt[sl, cc * CHUNK:(cc + 1) * CHUNK]
            vt_out[0, h, cc, MLA_V:MLA_V + ONES_ROWS, :] = _ones_rows(CHUNK)


def _mla_prep(z, pos3, qg, kvg, wqa, wqb, wk, wvt, pa, pb, frq, tm=512):
    bsz, seq, _ = z.shape
    full = lambda shape: pl.BlockSpec(shape, lambda b, i: (0,) * len(shape))
    qk_shape = jax.ShapeDtypeStruct((bsz, MLA_HEADS, seq, LANES), BF16)
    qk_spec = pl.BlockSpec((1, MLA_HEADS, tm, LANES), lambda b, i: (b, 0, i, 0))
    qt_shape = jax.ShapeDtypeStruct((bsz, MLA_HEADS, LANES, seq), BF16)
    qt_spec = pl.BlockSpec((1, MLA_HEADS, LANES, tm), lambda b, i: (b, 0, 0, i))
    return pl.pallas_call(
        _mla_prep_kernel,
        grid=(bsz, seq // tm),
        in_specs=[pl.BlockSpec((1, tm, MLA_Q_RANK), lambda b, i: (b, i, ZC_CQ // MLA_Q_RANK)),
                  pl.BlockSpec((1, tm, MLA_KV_RANK), lambda b, i: (b, i, ZC_CKV // MLA_KV_RANK)),
                  pl.BlockSpec((1, tm, LANES), lambda b, i: (b, i, ZC_MISC // LANES)),
                  pl.BlockSpec((1, 1, tm), lambda b, i: (b, 0, i)),
                  full((1, MLA_Q_RANK)), full((1, MLA_KV_RANK)),
                  full((MLA_HEADS * LANES, MLA_Q_RANK)), full((MLA_HEADS * LANES, MLA_Q_RANK)),
                  full((MLA_KV_RANK, MLA_HEADS * LANES)), full((MLA_HEADS * MLA_V, MLA_KV_RANK)),
                  full((LANES, LANES)), full((LANES, LANES)), full((MLA_ROPE // 2, 1))],
        out_specs=[qt_spec, qk_spec,
                   pl.BlockSpec((1, MLA_HEADS, tm // CHUNK, MLA_V + ONES_ROWS, CHUNK),
                                lambda b, i: (b, 0, i, 0, 0))],
        out_shape=[qt_shape, qk_shape,
                   jax.ShapeDtypeStruct((bsz, MLA_HEADS, seq // CHUNK, MLA_V + ONES_ROWS, CHUNK),
                                        BF16)],
        compiler_params=_cparams(2),
        name="mla_prep",
    )(z, z, z, pos3, qg, kvg, wqa, wqb, wk, wvt, pa, pb, frq)


def _mla_attn_kernel(q_ref, k_ref, vt_ref, o_ref, ot_scr, s_scr, mx_scr, m_scr, acc_scr):
    i = pl.program_id(1)
    causal = (lax.broadcasted_iota(jnp.int32, (CHUNK, CHUNK), 0)
              <= lax.broadcasted_iota(jnp.int32, (CHUNK, CHUNK), 1))

    heads = range(MLA_HEADS)
    for h in heads:
        _flash_reset(m_scr.at[h], acc_scr.at[h])

    def produce(h, c, slot, kind):
        off = pl.multiple_of(c * CHUNK, CHUNK)
        s = jnp.dot(k_ref[0, h, pl.ds(off, CHUNK), :], q_ref[0, h],
                    preferred_element_type=F32)
        if kind == "last":
            s = jnp.where(causal, s, NEG)
        elif kind == "first":
            s = jnp.where(jnp.logical_or(causal, c < i), s, NEG)
        _flash_produce(s_scr.at[slot, h], mx_scr.at[slot, h], s)

    def consume(h, c, slot):
        _flash_consume(s_scr.at[slot, h], mx_scr.at[slot, h], vt_ref[0, h, c],
                       m_scr.at[h], acc_scr.at[h])

    _run_flash(0, i, heads, produce, consume)
    for h in heads:
        ot_scr[h * MLA_V:(h + 1) * MLA_V, :] = (
            acc_scr[h, 0:MLA_V, :] / acc_scr[h, MLA_V:MLA_V + 1, :])
    o_ref[0] = ot_scr[...].T.astype(BF16)


def _mla_attn(qt, k, vt):
    bsz, nh, seq, _ = k.shape
    hpt = nh
    return pl.pallas_call(
        _mla_attn_kernel,
        grid=(bsz, seq // CHUNK),
        in_specs=[pl.BlockSpec((1, nh, LANES, CHUNK), lambda b, i: (b, 0, 0, i)),
                  pl.BlockSpec((1, nh, seq, LANES), lambda b, i: (b, 0, 0, 0)),
                  pl.BlockSpec((1, nh, seq // CHUNK, MLA_V + ONES_ROWS, CHUNK),
                               lambda b, i: (b, 0, 0, 0, 0))],
        out_specs=pl.BlockSpec((1, CHUNK, nh * MLA_V), lambda b, i: (b, i, 0)),
        out_shape=jax.ShapeDtypeStruct((bsz, seq, nh * MLA_V), BF16),
        scratch_shapes=[pltpu.VMEM((nh * MLA_V, CHUNK), F32),
                        pltpu.VMEM((2, hpt, CHUNK, CHUNK), F32),
                        pltpu.VMEM((2, hpt, 1, CHUNK), F32),
                        pltpu.VMEM((hpt, 1, CHUNK), F32),
                        pltpu.VMEM((hpt, MLA_V + ONES_ROWS, CHUNK), F32)],
        compiler_params=_cparams(2),
        name="mla_attn",
    )(qt, k, vt)


def _merge_ffn_kernel(x_ref, mod_ref, oc_ref, os_ref, ow_ref, om_ref, ma_ref, mb_ref, wo_ref,
                      g_ref, wg_ref, wu_ref, cw_ref, cb_ref, wd_ref, fg_ref,
                      o_ref, prev_ref, *, tm):
    i = pl.program_id(1)

    @pl.when(i == 0)
    def _():
        prev_ref[...] = jnp.zeros_like(prev_ref)

    mod = mod_ref[0]
    o_nsa = oc_ref[0].astype(F32) + os_ref[0].astype(F32) + ow_ref[0].astype(F32)
    y = (_sigmoid(ma_ref[0].astype(F32)) * o_nsa
         + _sigmoid(mb_ref[0].astype(F32)) * om_ref[0].astype(F32))
    a = jnp.dot(y.astype(BF16), wo_ref[...], preferred_element_type=F32)
    x1 = x_ref[0] + mod[:, 2 * D_MODEL:3 * D_MODEL] * a
    h2 = _modulated_norm(x1, g_ref[...], mod[:, 3 * D_MODEL:4 * D_MODEL],
                         mod[:, 4 * D_MODEL:5 * D_MODEL]).astype(BF16)
    chunks = [slice(f0, f1) for f0, f1 in zip(FFN_COL_SPLITS[:-1], FFN_COL_SPLITS[1:])]
    gates = [jnp.dot(h2, wg_ref[:, fs], preferred_element_type=F32) for fs in chunks]
    ups = [jnp.dot(h2, wu_ref[:, fs], preferred_element_type=F32) for fs in chunks]
    acc = jnp.zeros((tm, D_MODEL), F32)
    for fs, gt, up in zip(chunks, gates, ups):
        row = lax.broadcasted_iota(jnp.int32, gt.shape, 0)
        p1 = prev_ref[7:8, fs]
        p2 = prev_ref[6:7, fs]
        g1 = jnp.where(row == 0, p1, pltpu.roll(gt, 1, 0))
        g2 = jnp.where(row == 0, p2, jnp.where(row == 1, p1, pltpu.roll(gt, 2, 0)))
        prev_ref[:, fs] = gt[tm - 8:tm, :]
        conv = cb_ref[:, fs] + cw_ref[0:1, fs] * g2 + cw_ref[1:2, fs] * g1 + cw_ref[2:3, fs] * gt
        act = (conv * _sigmoid(conv)) * up
        acc = acc + jnp.dot(act.astype(BF16), wd_ref[fs, :], preferred_element_type=F32)
    x2 = x1 + mod[:, 5 * D_MODEL:6 * D_MODEL] * acc
    ms = jnp.mean(x2 * x2, axis=-1, keepdims=True)
    o_ref[0] = x2 * lax.rsqrt(ms + RMS_EPS) * fg_ref[...]


def _merge_ffn(x, mod3, o_cmp, o_slc, o_win, o_mla, z, w_o, gain, wg, wu, cw, cb, wd, fg,
               tm=512):
    bsz, seq, _ = x.shape
    tok = lambda col: pl.BlockSpec((1, tm, D_MODEL), lambda b, i: (b, i, col))
    return pl.pallas_call(
        functools.partial(_merge_ffn_kernel, tm=tm),
        grid=(bsz, seq // tm),
        in_specs=[tok(0),
                  pl.BlockSpec((1, 1, 6 * D_MODEL), lambda b, i: (b, 0, 0)),
                  tok(0), tok(0), tok(0), tok(0),
                  tok(ZC_MA // D_MODEL), tok(ZC_MB // D_MODEL),
                  _const_spec((D_MODEL, D_MODEL)), _const_spec((1, D_MODEL)),
                  _const_spec((D_MODEL, D_FF)), _const_spec((D_MODEL, D_FF)),
                  _const_spec((3, D_FF)), _const_spec((1, D_FF)), _const_spec((D_FF, D_MODEL)),
                  _const_spec((1, D_MODEL))],
        out_specs=tok(0),
        out_shape=jax.ShapeDtypeStruct((bsz, seq, D_MODEL), F32),
        scratch_shapes=[pltpu.VMEM((8, D_FF), F32)],
        compiler_params=_cparams(2),
        name="merge_ffn",
    )(x, mod3, o_cmp, o_slc, o_win, o_mla, z, z, w_o, gain, wg, wu, cw, cb, wd, fg)


def _prep_w_in(w_in):
    off = {}
    o = 0
    for name, width in (("nsa_q", 1024), ("k_cmp", 256), ("v_cmp", 256), ("k_slc", 256),
                        ("v_slc", 256), ("k_win", 256), ("v_win", 256), ("nsa_gate", 48),
                        ("mla_cq", 256), ("mla_ckv", 128), ("mla_krope", 32),
                        ("merge_a", 1024), ("merge_b", 1024)):
        off[name] = (o, o + width)
        o += width
    col = lambda name: w_in[:, off[name][0]:off[name][1]]
    q = col("nsa_q") * (NSA_D ** -0.5 * LOG2E)
    pad = jnp.zeros((D_MODEL, LANES - MLA_ROPE - 3 * NSA_HEADS), w_in.dtype)
    kv_cmp = jnp.stack([col("k_cmp").reshape(D_MODEL, NSA_GROUPS, NSA_D),
                        col("v_cmp").reshape(D_MODEL, NSA_GROUPS, NSA_D)],
                       axis=2).reshape(D_MODEL, CMP_WIDTH)
    parts = [col("merge_a"), col("merge_b"), col("k_slc"), col("v_slc"), col("k_win"),
             col("v_win"), col("mla_cq"), col("mla_ckv"), col("mla_krope"), col("nsa_gate"),
             pad, kv_cmp]
    return jnp.concatenate(parts, axis=1).astype(BF16), q.T.astype(BF16)


def _prep_mla_weights(w_uq, w_ukv):
    half = MLA_ROPE // 2
    dq = MLA_NOPE + MLA_ROPE
    wq = w_uq.reshape(MLA_Q_RANK, MLA_HEADS, dq)
    nope, x1, x2 = wq[..., :MLA_NOPE], wq[..., MLA_NOPE:MLA_NOPE + half], wq[..., MLA_NOPE + half:]
    z32 = jnp.zeros((MLA_Q_RANK, MLA_HEADS, LANES - dq), w_uq.dtype)
    wqa = jnp.concatenate([nope, x1, x2, z32], axis=-1)
    wqb = jnp.concatenate([jnp.zeros_like(nope), -x2, x1, z32], axis=-1)
    wkv = w_ukv.reshape(MLA_KV_RANK, MLA_HEADS, MLA_NOPE + MLA_V)
    wk = jnp.concatenate([wkv[..., :MLA_NOPE],
                          jnp.zeros((MLA_KV_RANK, MLA_HEADS, LANES - MLA_NOPE), w_ukv.dtype)],
                         axis=-1)
    wv = wkv[..., MLA_NOPE:]
    flat = lambda w: w.reshape(w.shape[0], MLA_HEADS * LANES).astype(BF16)
    return flat(wqa).T, flat(wqb).T, flat(wk), flat(wv).T


def _rope_constants():
    half = MLA_ROPE // 2
    pa = np.zeros((LANES, LANES), np.float32)
    pb = np.zeros((LANES, LANES), np.float32)
    for j in range(half):
        pa[j, MLA_NOPE + j] = 1.0
        pa[half + j, MLA_NOPE + half + j] = 1.0
        pb[half + j, MLA_NOPE + j] = -1.0
        pb[j, MLA_NOPE + half + j] = 1.0
    inv_freq = ROPE_THETA ** (-jnp.arange(0, MLA_ROPE, 2, dtype=F32) / MLA_ROPE)
    return jnp.asarray(pa, BF16), jnp.asarray(pb, BF16), inv_freq.reshape(half, 1)


def _overlap(seq):
    nc = (seq - CMP_LEN) // CMP_STRIDE + 1
    nb = seq // SEL_BLOCK
    cs = np.arange(nc) * CMP_STRIDE
    bs = np.arange(nb) * SEL_BLOCK
    ov = np.clip(np.minimum(cs[:, None] + CMP_LEN, bs[None, :] + SEL_BLOCK)
                 - np.maximum(cs[:, None], bs[None, :]), 0, None) / CMP_LEN
    out = np.zeros((nb, N_CMP_PAD), np.float32)
    out[:, :nc] = ov.T
    return jnp.asarray(out)


def kernel(x, c, positions, rel_bias_table, ada_w, ada_b, norm_mix_g, w_in, cmp_pos_k, cmp_w1_k, cmp_w2_k, cmp_pos_v, cmp_w1_v, cmp_w2_v, mla_q_norm_g, mla_w_uq, mla_kv_norm_g, mla_w_ukv, w_o, norm_ffn_g, ffn_w_gate, ffn_w_up, ffn_conv_w, ffn_conv_b, ffn_w_down, final_norm_g):
    bsz, seq, _ = x.shape
    assert ada_w.shape[0] == 1 and seq == N_SEL_BLOCKS * SEL_BLOCK
    n_back = WINDOW // CHUNK

    bias_c = _cmp_bias_table(rel_bias_table, seq)
    bias_s = _chunk_bias_table(rel_bias_table, limit=None, sub_far=True, n_tiles=3)
    bias_w = _chunk_bias_table(rel_bias_table, limit=WINDOW, sub_far=False, n_tiles=3)

    mod3 = _ada(c, ada_w[0], ada_b[0]).reshape(bsz, 1, 6 * D_MODEL)
    z, qt, cmp_in = _inproj(x, mod3, norm_mix_g, *_prep_w_in(w_in[0]))

    kvc, kvct = _compress(cmp_in,
                          jnp.stack([cmp_pos_k[0].reshape(1, -1), cmp_pos_v[0].reshape(1, -1)]),
                          jnp.stack([cmp_w1_k[0], cmp_w1_v[0]]),
                          jnp.stack([cmp_w2_k[0], cmp_w2_v[0]]))

    o_cmp, mb = _cmp_attn(z, qt, kvc, kvct, bias_c, _overlap(seq))
    o_slc = _nsa_flash(z, qt, mb, bias_s, k_col=ZC_KS, v_col=ZC_VS, n_back=None, gate_idx=1)
    o_win = _nsa_flash(z, qt, None, bias_w, k_col=ZC_KW, v_col=ZC_VW, n_back=n_back,
                       gate_idx=2)

    wqa, wqb, wk, wvt = _prep_mla_weights(mla_w_uq[0], mla_w_ukv[0])
    pa, pb, frq = _rope_constants()
    q_m, k_m, vt_m = _mla_prep(z, positions.reshape(bsz, 1, seq), mla_q_norm_g, mla_kv_norm_g,
                               wqa, wqb, wk, wvt, pa, pb, frq)
    o_mla = _mla_attn(q_m, k_m, vt_m)

    return _merge_ffn(x, mod3, o_cmp, o_slc, o_win, o_mla, z, w_o[0].astype(BF16), norm_ffn_g,
                      ffn_w_gate[0].astype(BF16), ffn_w_up[0].astype(BF16),
                      ffn_conv_w[0], ffn_conv_b, ffn_w_down[0].astype(BF16),
                      final_norm_g.reshape(1, D_MODEL))
```

```python
import functools
import math

import numpy as np
import jax
import jax.numpy as jnp
from jax import lax
from jax.experimental import pallas as pl
from jax.experimental.pallas import tpu as pltpu

F32 = jnp.float32
BF16 = jnp.bfloat16
HIGHEST = lax.Precision.HIGHEST

D_MODEL = 1024
NSA_HEADS = 16
NSA_GROUPS = 4
NSA_HPG = 4
NSA_D = 64
CMP_LEN = 32
CMP_STRIDE = 16
CMP_HIDDEN = 256
SEL_BLOCK = 64
SEL_TOPK = 16
WINDOW = 512
MLA_HEADS = 8
MLA_Q_RANK = 256
MLA_KV_RANK = 128
MLA_NOPE = 64
MLA_ROPE = 32
MLA_V = 128
ROPE_THETA = 10000.0
REL_BUCKETS = 32
REL_MAX_DIST = 128
D_FF = 2816
RMS_EPS = 1e-6
NEG = -1e30
LOG2E = math.log2(math.e)

LANES = 128
CHUNK = 256
CMP_TQ = 512
N_CMP_PAD = 128
N_SEL_BLOCKS = 32
ONES_ROWS = 16
FFN_COL_SPLITS = (0, 1536, D_FF)

ZC_MA = 0
ZC_MB = 1024
ZC_KS = 2048
ZC_VS = 2304
ZC_KW = 2560
ZC_VW = 2816
ZC_CQ = 3072
ZC_CKV = 3328
ZC_MISC = 3456
Z_WIDTH = 3584
CMP_WIDTH = 512
QT_ROWS = NSA_HEADS * LANES
GATE_LANE0 = MLA_ROPE
MASK_ROW0 = NSA_D

VMEM_LIMIT = 56 * 1024 * 1024

_NT = (((1,), (1,)), ((), ()))


def _cparams(n_axes):
    return pltpu.CompilerParams(
        dimension_semantics=("arbitrary",) * n_axes,
        vmem_limit_bytes=VMEM_LIMIT)


def _sigmoid(x):
    return 1.0 / (1.0 + jnp.exp(-x))


def _const_spec(shape):
    return pl.BlockSpec(shape, lambda *_: (0,) * len(shape), pipeline_mode=pl.Buffered(1))


def _t5_bucket(dist):
    n = jnp.maximum(dist, 0)
    exact = REL_BUCKETS // 2
    nf = jnp.maximum(n, exact).astype(F32)
    log_ratio = jnp.log(nf / exact) / math.log(REL_MAX_DIST / exact)
    large = jnp.minimum(exact + (log_ratio * (REL_BUCKETS - exact)).astype(jnp.int32),
                        REL_BUCKETS - 1)
    return jnp.where(n < exact, n, large)


def _bias_kernel(rel_ref, out_ref, *, rows, cols, row_coef, offset, limit, sub_far,
                 group_lanes, n_real):
    m = pl.program_id(0)
    r = lax.broadcasted_iota(jnp.int32, (rows, cols), 0)
    c = lax.broadcasted_iota(jnp.int32, (rows, cols), 1)
    dist = m * cols + c + row_coef * r + offset
    valid = (dist >= 0) & (m < n_real)
    if limit is not None:
        valid = valid & (dist < limit)
    bucket = _t5_bucket(dist)
    for h in range(NSA_HEADS):
        val = jnp.zeros((rows, cols), F32)
        for b in range(REL_BUCKETS):
            val = jnp.where(bucket == b, rel_ref[b, h], val)
        if sub_far:
            val = val - rel_ref[REL_BUCKETS - 1, h]
        val = jnp.where(valid, val * LOG2E, NEG)
        if group_lanes:
            hh = h % NSA_HPG
            out_ref[0, h // NSA_HPG, :, hh * cols:(hh + 1) * cols] = val
        else:
            out_ref[h] = val


def _cmp_bias_table(rel, seq):
    return pl.pallas_call(
        functools.partial(_bias_kernel, rows=N_CMP_PAD, cols=CMP_TQ, row_coef=-CMP_STRIDE,
                          offset=-(CMP_LEN - 1), limit=None, sub_far=False, group_lanes=False,
                          n_real=seq // CMP_TQ),
        grid=(seq // CMP_TQ,),
        in_specs=[pl.BlockSpec(memory_space=pltpu.SMEM)],
        out_specs=pl.BlockSpec((NSA_HEADS, N_CMP_PAD, CMP_TQ), lambda m: (0, 0, m)),
        out_shape=jax.ShapeDtypeStruct((NSA_HEADS, N_CMP_PAD, seq), F32),
        compiler_params=_cparams(1),
        name="bias_cmp",
    )(rel)


def _chunk_bias_table(rel, *, limit, sub_far, n_tiles):
    return pl.pallas_call(
        functools.partial(_bias_kernel, rows=CHUNK, cols=CHUNK, row_coef=-1, offset=0,
                          limit=limit, sub_far=sub_far, group_lanes=True, n_real=3),
        grid=(n_tiles,),
        in_specs=[pl.BlockSpec(memory_space=pltpu.SMEM)],
        out_specs=pl.BlockSpec((1, NSA_GROUPS, CHUNK, NSA_HPG * CHUNK), lambda m: (m, 0, 0, 0)),
        out_shape=jax.ShapeDtypeStruct((n_tiles, NSA_GROUPS, CHUNK, NSA_HPG * CHUNK), F32),
        compiler_params=_cparams(1),
        name="bias_chunk",
    )(rel)


def _ada_kernel(c_ref, w_ref, b_ref, o_ref):
    c = c_ref[...]
    cond = c * _sigmoid(c)
    o_ref[...] = jnp.dot(cond, w_ref[...], precision=HIGHEST,
                         preferred_element_type=F32) + b_ref[...]


def _ada(c, w, b):
    bsz = c.shape[0]
    n = w.shape[1]
    tn = 1024
    return pl.pallas_call(
        _ada_kernel,
        grid=(n // tn,),
        in_specs=[pl.BlockSpec((bsz, D_MODEL), lambda j: (0, 0)),
                  pl.BlockSpec((D_MODEL, tn), lambda j: (0, j)),
                  pl.BlockSpec((1, tn), lambda j: (0, j))],
        out_specs=pl.BlockSpec((bsz, tn), lambda j: (0, j)),
        out_shape=jax.ShapeDtypeStruct((bsz, n), F32),
        compiler_params=_cparams(1),
        name="ada_mod",
    )(c, w, b.reshape(1, n))


def _modulated_norm(x, gain, shift, scale):
    ms = jnp.mean(x * x, axis=-1, keepdims=True)
    return (x * lax.rsqrt(ms + RMS_EPS) * gain) * (1.0 + scale) + shift


def _inproj_kernel(x_ref, mod_ref, g_ref, w_ref, wq_ref, z_ref, qt_ref, cmp_ref, *, tn):
    mod = mod_ref[0]
    h = _modulated_norm(x_ref[0], g_ref[...], mod[:, 0:D_MODEL],
                        mod[:, D_MODEL:2 * D_MODEL])
    hb = h.astype(BF16)
    tm = hb.shape[0]
    qt = lax.dot_general(wq_ref[...], hb, _NT, preferred_element_type=F32)
    zeros = jnp.zeros((LANES - NSA_D, tm), BF16)
    for hd in range(NSA_HEADS):
        qt_ref[0, hd * LANES:hd * LANES + NSA_D, :] = (
            qt[hd * NSA_D:(hd + 1) * NSA_D, :].astype(BF16))
        qt_ref[0, hd * LANES + NSA_D:(hd + 1) * LANES, :] = zeros
    for n0 in range(0, Z_WIDTH, tn):
        z_ref[0, :, n0:n0 + tn] = jnp.dot(
            hb, w_ref[:, n0:n0 + tn], preferred_element_type=F32).astype(BF16)
    acc = jnp.dot(hb, w_ref[:, Z_WIDTH:Z_WIDTH + CMP_WIDTH], preferred_element_type=F32)
    for g in range(NSA_GROUPS):
        cmp_ref[0, g] = acc[:, g * LANES:(g + 1) * LANES]


def _inproj(x, mod3, gain, w_p, wq_t, tm=512):
    bsz, seq, _ = x.shape
    return pl.pallas_call(
        functools.partial(_inproj_kernel, tn=512),
        grid=(bsz, seq // tm),
        in_specs=[pl.BlockSpec((1, tm, D_MODEL), lambda b, i: (b, i, 0)),
                  pl.BlockSpec((1, 1, 6 * D_MODEL), lambda b, i: (b, 0, 0)),
                  pl.BlockSpec((1, D_MODEL), lambda b, i: (0, 0)),
                  _const_spec(w_p.shape), _const_spec(wq_t.shape)],
        out_specs=[pl.BlockSpec((1, tm, Z_WIDTH), lambda b, i: (b, i, 0)),
                   pl.BlockSpec((1, QT_ROWS, tm), lambda b, i: (b, 0, i)),
                   pl.BlockSpec((1, NSA_GROUPS, tm, LANES), lambda b, i: (b, 0, i, 0))],
        out_shape=[jax.ShapeDtypeStruct((bsz, seq, Z_WIDTH), BF16),
                   jax.ShapeDtypeStruct((bsz, QT_ROWS, seq), BF16),
                   jax.ShapeDtypeStruct((bsz, NSA_GROUPS, seq, LANES), F32)],
        compiler_params=_cparams(2),
        name="in_proj",
    )(x, mod3, gain, w_p, wq_t)


def _gelu_tanh(x):
    return 0.5 * x * (1.0 + jnp.tanh(math.sqrt(2.0 / math.pi) * (x + 0.044715 * (x * x * x))))


def _pos_term_kernel(pos_ref, w1_ref, o_ref):
    pos = jnp.broadcast_to(pos_ref[0], (8, CMP_LEN * NSA_D))
    o_ref[0] = jnp.dot(pos, w1_ref[0], precision=HIGHEST, preferred_element_type=F32)


def _pos_term(pos, w1):
    return pl.pallas_call(
        _pos_term_kernel,
        grid=(2,),
        in_specs=[pl.BlockSpec((1, 1, CMP_LEN * NSA_D), lambda j: (j, 0, 0)),
                  pl.BlockSpec((1, CMP_LEN * NSA_D, CMP_HIDDEN), lambda j: (j, 0, 0))],
        out_specs=pl.BlockSpec((1, 8, CMP_HIDDEN), lambda j: (j, 0, 0)),
        out_shape=jax.ShapeDtypeStruct((2, 8, CMP_HIDDEN), F32),
        compiler_params=_cparams(1),
        name="cmp_pos_term",
    )(pos, w1)


def _compress_kernel(y_ref, pterm_ref, w1_ref, w2_ref, w2t_ref, kc_ref, vct_ref):
    nhb = y_ref.shape[2] // CMP_STRIDE
    low = lax.broadcasted_iota(jnp.int32, (nhb, LANES), 1) < NSA_D
    k_rows, v_rows = [], []
    for g in range(NSA_GROUPS):
        k_cols, v_cols = [], []
        for r in range(0, CMP_STRIDE, 2):
            a = y_ref[0, g, pl.ds(r, nhb, stride=CMP_STRIDE), :]
            b = y_ref[0, g, pl.ds(r + 1, nhb, stride=CMP_STRIDE), :]
            k_cols.append(jnp.where(low, a, pltpu.roll(b, NSA_D, 1)))
            v_cols.append(jnp.where(low, pltpu.roll(a, NSA_D, 1), b))
        k_rows.append(jnp.concatenate(k_cols, axis=1).astype(BF16))
        v_rows.append(jnp.concatenate(v_cols, axis=1).astype(BF16))
    half = (CMP_LEN // 2) * NSA_D
    outs = []
    for kv, rows in enumerate((k_rows, v_rows)):
        y = jnp.concatenate(rows, axis=0)
        top = jnp.dot(y, w1_ref[kv, 0:half, :], preferred_element_type=F32)
        bot = jnp.dot(y, w1_ref[kv, half:2 * half, :], preferred_element_type=F32)
        hidden = top + pltpu.roll(bot, NSA_GROUPS * nhb - 1, 0) + pterm_ref[kv, 0:1, :]
        outs.append(_gelu_tanh(hidden).astype(BF16))
    out = jnp.dot(outs[0], w2_ref[0], preferred_element_type=F32).astype(BF16)
    out_t = lax.dot_general(w2t_ref[1], outs[1], _NT,
                            preferred_element_type=F32).astype(BF16)
    for g in range(NSA_GROUPS):
        og = out[g * nhb:(g + 1) * nhb]
        kc_ref[0, g] = jnp.concatenate([og, jnp.zeros_like(og)], axis=1)
        vct_ref[0, g] = out_t[:, g * nhb:(g + 1) * nhb]


def _compress(cmp_in, pos, w1, w2):
    bsz, ng, seq, _ = cmp_in.shape
    nhb = seq // CMP_STRIDE
    w2b = w2.astype(BF16)
    return pl.pallas_call(
        _compress_kernel,
        grid=(bsz,),
        in_specs=[pl.BlockSpec((1, ng, seq, LANES), lambda b: (b, 0, 0, 0)),
                  _const_spec((2, 8, CMP_HIDDEN)),
                  _const_spec((2, CMP_LEN * NSA_D, CMP_HIDDEN)),
                  _const_spec((2, CMP_HIDDEN, NSA_D)),
                  _const_spec((2, NSA_D, CMP_HIDDEN))],
        out_specs=[pl.BlockSpec((1, ng, nhb, LANES), lambda b: (b, 0, 0, 0)),
                   pl.BlockSpec((1, ng, NSA_D, nhb), lambda b: (b, 0, 0, 0))],
        out_shape=[jax.ShapeDtypeStruct((bsz, ng, nhb, LANES), BF16),
                   jax.ShapeDtypeStruct((bsz, ng, NSA_D, nhb), BF16)],
        compiler_params=_cparams(1),
        name="compress",
    )(cmp_in, _pos_term(pos, w1), w1.astype(BF16), w2b, w2b.transpose(0, 2, 1))


def _select_mask(score, jj, cur):
    forced = (jj == 0) | (jj == cur) | (jj == cur - 1)
    x = jnp.where(forced, jnp.inf, jnp.where(jj > cur, -jnp.inf, score))
    below = pltpu.bitcast(pltpu.bitcast(x, jnp.int32) - 1, F32)
    x_lo = jnp.where(x > 0.0, below, jnp.where(x == 0.0, -1.0, x))
    tq = x.shape[1]
    n_sub = 8
    masks = []
    for v in range(N_SEL_BLOCKS // n_sub):
        rows = slice(v * n_sub, (v + 1) * n_sub)
        xv, xlv = x[rows], x_lo[rows]
        jv = v * n_sub + lax.broadcasted_iota(jnp.int32, (n_sub, tq), 0)
        rank = jnp.zeros((n_sub, tq), F32)
        for j2 in range(N_SEL_BLOCKS):
            row = x[j2:j2 + 1, :]
            if v * n_sub > j2:
                thr = xlv
            elif (v + 1) * n_sub - 1 <= j2:
                thr = xv
            else:
                thr = jnp.where(jv > j2, xlv, xv)
            rank = rank + jnp.where(row > thr, 1.0, 0.0)
        masks.append(jnp.where(rank < SEL_TOPK, 0.0, NEG))
    return jnp.concatenate(masks, axis=0)


def _cmp_attn_kernel(q_ref, misc_ref, kc_ref, vct_ref, bias_ref, ovl_ref, o_ref, mb_ref,
                     gt_scr, ot_scr):
    i = pl.program_id(1)
    tq = CMP_TQ
    gt_scr[...] = _sigmoid(misc_ref[0].astype(F32)).T
    jj = lax.broadcasted_iota(jnp.int32, (N_SEL_BLOCKS, tq), 0)
    tt = i * tq + lax.broadcasted_iota(jnp.int32, (N_SEL_BLOCKS, tq), 1)
    cur = tt // SEL_BLOCK
    t_row = i * tq + lax.broadcasted_iota(jnp.int32, (1, tq), 1)
    sees_any = t_row >= CMP_LEN - 1
    logits = [jnp.dot(kc_ref[0, h // NSA_HPG], q_ref[0, h * LANES:(h + 1) * LANES, :],
                      preferred_element_type=F32) + bias_ref[h]
              for h in range(NSA_HEADS)]
    for g in range(NSA_GROUPS):
        vct = vct_ref[0, g]
        psum = jnp.zeros((N_CMP_PAD, tq), F32)
        for hh in range(NSA_HPG):
            h = g * NSA_HPG + hh
            s = logits[h]
            e = jnp.exp2(s - jnp.max(s, axis=0, keepdims=True))
            inv = jnp.where(sees_any, 1.0 / jnp.sum(e, axis=0, keepdims=True), 0.0)
            p = e * inv
            psum = psum + p
            o = jnp.dot(vct, p.astype(BF16), preferred_element_type=F32)
            lane = GATE_LANE0 + 3 * h
            ot_scr[h * NSA_D:(h + 1) * NSA_D, :] = o * gt_scr[lane:lane + 1, :]
        score = jnp.dot(ovl_ref[...], psum, precision=HIGHEST,
                        preferred_element_type=F32)
        mb_ref[0, g * LANES:(g + 1) * LANES, :] = jnp.concatenate(
            [jnp.zeros((MASK_ROW0, tq), F32), _select_mask(score, jj, cur),
             jnp.zeros((LANES - MASK_ROW0 - N_SEL_BLOCKS, tq), F32)], axis=0).astype(BF16)
    o_ref[0] = ot_scr[...].T.astype(BF16)


def _cmp_attn(z, qt, kc, vct, bias_c, ovl):
    bsz, seq, _ = z.shape
    tq = CMP_TQ
    return pl.pallas_call(
        _cmp_attn_kernel,
        grid=(bsz, seq // tq),
        in_specs=[pl.BlockSpec((1, QT_ROWS, tq), lambda b, i: (b, 0, i)),
                  pl.BlockSpec((1, tq, LANES), lambda b, i: (b, i, ZC_MISC // LANES)),
                  pl.BlockSpec((1, NSA_GROUPS, N_CMP_PAD, LANES), lambda b, i: (b, 0, 0, 0)),
                  pl.BlockSpec((1, NSA_GROUPS, NSA_D, N_CMP_PAD), lambda b, i: (b, 0, 0, 0)),
                  pl.BlockSpec((NSA_HEADS, N_CMP_PAD, tq), lambda b, i: (0, 0, i)),
                  pl.BlockSpec((N_SEL_BLOCKS, N_CMP_PAD), lambda b, i: (0, 0))],
        out_specs=[pl.BlockSpec((1, tq, D_MODEL), lambda b, i: (b, i, 0)),
                   pl.BlockSpec((1, NSA_GROUPS * LANES, tq), lambda b, i: (b, 0, i))],
        out_shape=[jax.ShapeDtypeStruct((bsz, seq, D_MODEL), BF16),
                   jax.ShapeDtypeStruct((bsz, NSA_GROUPS * LANES, seq), BF16)],
        scratch_shapes=[pltpu.VMEM((LANES, tq), F32),
                        pltpu.VMEM((D_MODEL, tq), F32)],
        compiler_params=_cparams(2),
        name="cmp_attn_select",
    )(qt, z, kc, vct, bias_c, ovl)


def _flash_produce(s_ref, mx_ref, s):
    s_ref[...] = s
    mx_ref[...] = jnp.max(s, axis=0, keepdims=True)


def _flash_consume(s_ref, mx_ref, vt, m_ref, acc_ref):
    m_old = m_ref[...]
    m_new = jnp.maximum(m_old, mx_ref[...])
    alpha = jnp.exp2(m_old - m_new)
    p = jnp.exp2((s_ref[...] - m_new).astype(BF16))
    acc_ref[...] = alpha * acc_ref[...] + jnp.dot(vt, p, preferred_element_type=F32)
    m_ref[...] = m_new


def _flash_reset(m_ref, acc_ref):
    m_ref[...] = jnp.full(m_ref.shape, NEG, F32)
    acc_ref[...] = jnp.zeros(acc_ref.shape, F32)


def _ones_rows(n_cols):
    row = lax.broadcasted_iota(jnp.int32, (ONES_ROWS, n_cols), 0)
    return jnp.where(row == 0, 1.0, 0.0).astype(BF16)


def _run_flash(lo, hi, streams, produce_one, consume_one, common_n=None):
    n = hi - lo + 1

    def produce(c, slot, kind):
        for s in streams:
            produce_one(s, c, slot, kind)

    def consume(c, slot):
        for s in streams:
            consume_one(s, c, slot)

    def both(cp, sp, kind, cc, sc):
        for s in streams:
            produce_one(s, cp, sp, kind)
            consume_one(s, cc, sc)

    def generic():
        produce(lo, 0, "first")

        @pl.when(n % 2 == 1)
        def _():
            nxt = jnp.minimum(lo + 1, hi)
            for s in streams:
                consume_one(s, lo, 0)
                produce_one(s, nxt, 0, "first")

        start = lo + n % 2
        n_pairs = n // 2

        def pair(t, carry):
            c = start + 2 * t
            both(c + 1, 1, "far", c, 0)
            both(c + 2, 0, "trail", c + 1, 1)
            return carry

        lax.fori_loop(0, n_pairs - 1, pair, 0)

        @pl.when(n_pairs >= 1)
        def _():
            both(hi, 1, "last", hi - 1, 0)
            consume(hi, 1)

    if common_n is None:
        generic()
        return

    @pl.when(n == common_n)
    def _():
        produce(lo, 0, "first")
        for k in range(common_n - 1):
            both(lo + k + 1, (k + 1) % 2, "last" if k + 2 == common_n else "first",
                 lo + k, k % 2)
        consume(hi, (common_n - 1) % 2)

    @pl.when(n != common_n)
    def _():
        generic()


def _nsa_flash_kernel(*refs, use_mask, n_back, gate_idx):
    if use_mask:
        q_ref, mb_ref, *refs = refs
    else:
        q_ref, *refs = refs
    misc_ref, k_ref, v_ref, bias_ref, o_ref, kaug, vt, qs_scr, gt_scr, ot_scr, *flash_scr = refs
    i = pl.program_id(1)
    seq = k_ref.shape[1]

    @pl.when(i == 0)
    def _():
        if use_mask:
            blk = lax.broadcasted_iota(jnp.int32, (seq, NSA_D), 0) // SEL_BLOCK
            lane = lax.broadcasted_iota(jnp.int32, (seq, NSA_D), 1)
            extra = jnp.where(blk == lane, 1.0, 0.0).astype(BF16)
        else:
            extra = jnp.zeros((seq, NSA_D), BF16)
        eye = (lax.broadcasted_iota(jnp.int32, (NSA_D, NSA_D), 0)
               == lax.broadcasted_iota(jnp.int32, (NSA_D, NSA_D), 1)).astype(BF16)
        for g in range(NSA_GROUPS):
            gs = slice(g * NSA_D, (g + 1) * NSA_D)
            kaug[g] = jnp.concatenate([k_ref[0, :, gs], extra], axis=1)
            for c in range(seq // CHUNK):
                vt[g, c, 0:NSA_D, :] = lax.dot_general(
                    eye, v_ref[0, c * CHUNK:(c + 1) * CHUNK, gs], _NT,
                    preferred_element_type=F32).astype(BF16)
                vt[g, c, NSA_D:NSA_D + ONES_ROWS, :] = _ones_rows(CHUNK)

    gt_scr[...] = _sigmoid(misc_ref[0].astype(F32)).T
    for h in range(NSA_HEADS):
        g, hh = divmod(h, NSA_HPG)
        qh = q_ref[0, h * LANES:(h + 1) * LANES, :]
        if use_mask:
            qh = qh + mb_ref[0, g * LANES:(g + 1) * LANES, :]
        qs_scr[g, :, hh * CHUNK:(hh + 1) * CHUNK] = qh

    s_scr, mx_scr, m_scr, acc_scr = flash_scr
    heads = range(NSA_HEADS)

    def view(ref, h, *lead):
        g, hh = divmod(h, NSA_HPG)
        return ref.at[(*lead, g, slice(None), pl.ds(hh * CHUNK, CHUNK))]

    for h in heads:
        _flash_reset(view(m_scr, h), view(acc_scr, h))

    def produce(h, c, slot, kind):
        g = h // NSA_HPG
        off = pl.multiple_of(c * CHUNK, CHUNK)
        s = jnp.dot(kaug[g, pl.ds(off, CHUNK), :], view(qs_scr, h)[...],
                    preferred_element_type=F32)
        if kind == "last":
            s = s + view(bias_ref, h, 0)[...]
        elif kind != "far" or n_back is not None:
            s = s + view(bias_ref, h, jnp.minimum(i - c, 2))[...]
        _flash_produce(view(s_scr, h, slot), view(mx_scr, h, slot), s)

    def consume(h, c, slot):
        _flash_consume(view(s_scr, h, slot), view(mx_scr, h, slot), vt[h // NSA_HPG, c],
                       view(m_scr, h), view(acc_scr, h))

    if n_back is None:
        _run_flash(0, i, heads, produce, consume)
    else:
        _run_flash(jnp.maximum(i - n_back, 0), i, heads, produce, consume,
                   common_n=n_back + 1)
    for h in heads:
        acc = view(acc_scr, h)
        lane = GATE_LANE0 + 3 * h + gate_idx
        scale = gt_scr[lane:lane + 1, :] / acc[NSA_D:NSA_D + 1, :]
        ot_scr[h * NSA_D:(h + 1) * NSA_D, :] = acc[0:NSA_D, :] * scale
    o_ref[0] = ot_scr[...].T.astype(BF16)


def _nsa_flash(z, qt, mb, bias_t, *, k_col, v_col, n_back, gate_idx):
    bsz, seq, _ = z.shape
    use_mask = mb is not None
    wide = NSA_HPG * CHUNK
    in_specs = [pl.BlockSpec((1, QT_ROWS, CHUNK), lambda b, i: (b, 0, i))]
    args = [qt]
    if use_mask:
        in_specs.append(pl.BlockSpec((1, NSA_GROUPS * LANES, CHUNK), lambda b, i: (b, 0, i)))
        args.append(mb)
    in_specs += [pl.BlockSpec((1, CHUNK, LANES), lambda b, i: (b, i, ZC_MISC // LANES)),
                 pl.BlockSpec((1, seq, 256), lambda b, i: (b, 0, k_col // 256)),
                 pl.BlockSpec((1, seq, 256), lambda b, i: (b, 0, v_col // 256)),
                 _const_spec(bias_t.shape)]
    args += [z, z, z, bias_t]
    scratch = [pltpu.VMEM((NSA_GROUPS, seq, LANES), BF16),
               pltpu.VMEM((NSA_GROUPS, seq // CHUNK, NSA_D + ONES_ROWS, CHUNK), BF16),
               pltpu.VMEM((NSA_GROUPS, LANES, wide), BF16),
               pltpu.VMEM((LANES, CHUNK), F32),
               pltpu.VMEM((D_MODEL, CHUNK), F32),
               pltpu.VMEM((2, NSA_GROUPS, CHUNK, wide), F32),
               pltpu.VMEM((2, NSA_GROUPS, 1, wide), F32),
               pltpu.VMEM((NSA_GROUPS, 1, wide), F32),
               pltpu.VMEM((NSA_GROUPS, NSA_D + ONES_ROWS, wide), F32)]
    return pl.pallas_call(
        functools.partial(_nsa_flash_kernel, use_mask=use_mask, n_back=n_back,
                          gate_idx=gate_idx),
        grid=(bsz, seq // CHUNK),
        in_specs=in_specs,
        out_specs=pl.BlockSpec((1, CHUNK, D_MODEL), lambda b, i: (b, i, 0)),
        out_shape=jax.ShapeDtypeStruct((bsz, seq, D_MODEL), BF16),
        scratch_shapes=scratch,
        compiler_params=_cparams(2),
        name="slc_attn" if use_mask else "win_attn",
    )(*args)


def _rms(x, gain):
    ms = jnp.mean(x * x, axis=-1, keepdims=True)
    return x * lax.rsqrt(ms + RMS_EPS) * gain


def _mla_prep_kernel(cq_ref, ckv_ref, misc_ref, pos_ref, qg_ref, kvg_ref, wqa_ref, wqb_ref,
                     wk_ref, wvt_ref, pa_ref, pb_ref, frq_ref, q_out, k_out, vt_out):
    tm = cq_ref.shape[1]
    cqn = _rms(cq_ref[0].astype(F32), qg_ref[...]).astype(BF16)
    ckvn = _rms(ckv_ref[0].astype(F32), kvg_ref[...]).astype(BF16)
    half = MLA_ROPE // 2
    ang = frq_ref[...] * pos_ref[0].astype(F32)
    cos_t, sin_t = jnp.cos(ang), jnp.sin(ang)
    ones = jnp.ones((MLA_NOPE, tm), F32)
    zeros = jnp.zeros((LANES - MLA_NOPE - 2 * half, tm), F32)
    cos_rows = jnp.concatenate([ones, cos_t, cos_t, zeros], axis=0)
    sin_rows = jnp.concatenate([0.0 * ones, sin_t, sin_t, zeros], axis=0)
    cos, sin = cos_rows.T, sin_rows.T
    scale = (MLA_NOPE + MLA_ROPE) ** -0.5 * LOG2E
    qa = lax.dot_general(wqa_ref[...], cqn, _NT, preferred_element_type=F32)
    qb = lax.dot_general(wqb_ref[...], cqn, _NT, preferred_element_type=F32)
    misc = misc_ref[0]
    kr = (jnp.dot(misc, pa_ref[...], preferred_element_type=F32) * cos
          + jnp.dot(misc, pb_ref[...], preferred_element_type=F32) * sin)
    kn = jnp.dot(ckvn, wk_ref[...], preferred_element_type=F32)
    vt = lax.dot_general(wvt_ref[...], ckvn, _NT,
                         preferred_element_type=F32).astype(BF16)
    for h in range(MLA_HEADS):
        sl = slice(h * LANES, (h + 1) * LANES)
        q_out[0, h] = ((qa[sl, :] * cos_rows + qb[sl, :] * sin_rows) * scale).astype(BF16)
        k_out[0, h] = (kn[:, sl] + kr).astype(BF16)
        for cc in range(tm // CHUNK):
            vt_out[0, h, cc, 0:MLA_V, :] = vt[sl, cc * CHUNK:(cc + 1) * CHUNK]
            vt_out[0, h, cc, MLA_V:MLA_V + ONES_ROWS, :] = _ones_rows(CHUNK)


def _mla_prep(z, pos3, qg, kvg, wqa, wqb, wk, wvt, pa, pb, frq, tm=1024):
    bsz, seq, _ = z.shape
    full = lambda shape: pl.BlockSpec(shape, lambda b, i: (0,) * len(shape))
    qk_shape = jax.ShapeDtypeStruct((bsz, MLA_HEADS, seq, LANES), BF16)
    qk_spec = pl.BlockSpec((1, MLA_HEADS, tm, LANES), lambda b, i: (b, 0, i, 0))
    qt_shape = jax.ShapeDtypeStruct((bsz, MLA_HEADS, LANES, seq), BF16)
    qt_spec = pl.BlockSpec((1, MLA_HEADS, LANES, tm), lambda b, i: (b, 0, 0, i))
    return pl.pallas_call(
        _mla_prep_kernel,
        grid=(bsz, seq // tm),
        in_specs=[pl.BlockSpec((1, tm, MLA_Q_RANK), lambda b, i: (b, i, ZC_CQ // MLA_Q_RANK)),
                  pl.BlockSpec((1, tm, MLA_KV_RANK), lambda b, i: (b, i, ZC_CKV // MLA_KV_RANK)),
                  pl.BlockSpec((1, tm, LANES), lambda b, i: (b, i, ZC_MISC // LANES)),
                  pl.BlockSpec((1, 1, tm), lambda b, i: (b, 0, i)),
                  full((1, MLA_Q_RANK)), full((1, MLA_KV_RANK)),
                  full((MLA_HEADS * LANES, MLA_Q_RANK)), full((MLA_HEADS * LANES, MLA_Q_RANK)),
                  full((MLA_KV_RANK, MLA_HEADS * LANES)), full((MLA_HEADS * MLA_V, MLA_KV_RANK)),
                  full((LANES, LANES)), full((LANES, LANES)), full((MLA_ROPE // 2, 1))],
        out_specs=[qt_spec, qk_spec,
                   pl.BlockSpec((1, MLA_HEADS, tm // CHUNK, MLA_V + ONES_ROWS, CHUNK),
                                lambda b, i: (b, 0, i, 0, 0))],
        out_shape=[qt_shape, qk_shape,
                   jax.ShapeDtypeStruct((bsz, MLA_HEADS, seq // CHUNK, MLA_V + ONES_ROWS, CHUNK),
                                        BF16)],
        compiler_params=_cparams(2),
        name="mla_prep",
    )(z, z, z, pos3, qg, kvg, wqa, wqb, wk, wvt, pa, pb, frq)


def _mla_attn_kernel(q_ref, k_ref, vt_ref, o_ref, ot_scr, s_scr, mx_scr, m_scr, acc_scr):
    i = pl.program_id(1)
    causal = (lax.broadcasted_iota(jnp.int32, (CHUNK, CHUNK), 0)
              <= lax.broadcasted_iota(jnp.int32, (CHUNK, CHUNK), 1))

    heads = range(MLA_HEADS)
    for h in heads:
        _flash_reset(m_scr.at[h], acc_scr.at[h])

    def produce(h, c, slot, kind):
        off = pl.multiple_of(c * CHUNK, CHUNK)
        s = jnp.dot(k_ref[0, h, pl.ds(off, CHUNK), :], q_ref[0, h],
                    preferred_element_type=F32)
        if kind == "last":
            s = jnp.where(causal, s, NEG)
        elif kind == "first":
            s = jnp.where(jnp.logical_or(causal, c < i), s, NEG)
        _flash_produce(s_scr.at[slot, h], mx_scr.at[slot, h], s)

    def consume(h, c, slot):
        _flash_consume(s_scr.at[slot, h], mx_scr.at[slot, h], vt_ref[0, h, c],
                       m_scr.at[h], acc_scr.at[h])

    _run_flash(0, i, heads, produce, consume)
    for h in heads:
        ot_scr[h * MLA_V:(h + 1) * MLA_V, :] = (
            acc_scr[h, 0:MLA_V, :] / acc_scr[h, MLA_V:MLA_V + 1, :])
    o_ref[0] = ot_scr[...].T.astype(BF16)


def _mla_attn(qt, k, vt):
    bsz, nh, seq, _ = k.shape
    hpt = nh
    return pl.pallas_call(
        _mla_attn_kernel,
        grid=(bsz, seq // CHUNK),
        in_specs=[pl.BlockSpec((1, nh, LANES, CHUNK), lambda b, i: (b, 0, 0, i)),
                  pl.BlockSpec((1, nh, seq, LANES), lambda b, i: (b, 0, 0, 0)),
                  pl.BlockSpec((1, nh, seq // CHUNK, MLA_V + ONES_ROWS, CHUNK),
                               lambda b, i: (b, 0, 0, 0, 0))],
        out_specs=pl.BlockSpec((1, CHUNK, nh * MLA_V), lambda b, i: (b, i, 0)),
        out_shape=jax.ShapeDtypeStruct((bsz, seq, nh * MLA_V), BF16),
        scratch_shapes=[pltpu.VMEM((nh * MLA_V, CHUNK), F32),
                        pltpu.VMEM((2, hpt, CHUNK, CHUNK), F32),
                        pltpu.VMEM((2, hpt, 1, CHUNK), F32),
                        pltpu.VMEM((hpt, 1, CHUNK), F32),
                        pltpu.VMEM((hpt, MLA_V + ONES_ROWS, CHUNK), F32)],
        compiler_params=_cparams(2),
        name="mla_attn",
    )(qt, k, vt)


def _merge_ffn_kernel(x_ref, mod_ref, oc_ref, os_ref, ow_ref, om_ref, ma_ref, mb_ref, wo_ref,
                      g_ref, wg_ref, wu_ref, cw_ref, cb_ref, wd_ref, fg_ref,
                      o_ref, prev_ref, *, tm):
    i = pl.program_id(1)

    @pl.when(i == 0)
    def _():
        prev_ref[...] = jnp.zeros_like(prev_ref)

    mod = mod_ref[0]
    o_nsa = oc_ref[0].astype(F32) + os_ref[0].astype(F32) + ow_ref[0].astype(F32)
    y = (_sigmoid(ma_ref[0].astype(F32)) * o_nsa
         + _sigmoid(mb_ref[0].astype(F32)) * om_ref[0].astype(F32))
    a = jnp.dot(y.astype(BF16), wo_ref[...], preferred_element_type=F32)
    x1 = x_ref[0] + mod[:, 2 * D_MODEL:3 * D_MODEL] * a
    h2 = _modulated_norm(x1, g_ref[...], mod[:, 3 * D_MODEL:4 * D_MODEL],
                         mod[:, 4 * D_MODEL:5 * D_MODEL]).astype(BF16)
    chunks = [slice(f0, f1) for f0, f1 in zip(FFN_COL_SPLITS[:-1], FFN_COL_SPLITS[1:])]
    gates = [jnp.dot(h2, wg_ref[:, fs], preferred_element_type=F32) for fs in chunks]
    ups = [jnp.dot(h2, wu_ref[:, fs], preferred_element_type=F32) for fs in chunks]
    acc = jnp.zeros((tm, D_MODEL), F32)
    for fs, gt, up in zip(chunks, gates, ups):
        row = lax.broadcasted_iota(jnp.int32, gt.shape, 0)
        p1 = prev_ref[7:8, fs]
        p2 = prev_ref[6:7, fs]
        g1 = jnp.where(row == 0, p1, pltpu.roll(gt, 1, 0))
        g2 = jnp.where(row == 0, p2, jnp.where(row == 1, p1, pltpu.roll(gt, 2, 0)))
        prev_ref[:, fs] = gt[tm - 8:tm, :]
        conv = cb_ref[:, fs] + cw_ref[0:1, fs] * g2 + cw_ref[1:2, fs] * g1 + cw_ref[2:3, fs] * gt
        act = (conv * _sigmoid(conv)) * up
        acc = acc + jnp.dot(act.astype(BF16), wd_ref[fs, :], preferred_element_type=F32)
    x2 = x1 + mod[:, 5 * D_MODEL:6 * D_MODEL] * acc
    ms = jnp.mean(x2 * x2, axis=-1, keepdims=True)
    o_ref[0] = x2 * lax.rsqrt(ms + RMS_EPS) * fg_ref[...]


def _merge_ffn(x, mod3, o_cmp, o_slc, o_win, o_mla, z, w_o, gain, wg, wu, cw, cb, wd, fg,
               tm=512):
    bsz, seq, _ = x.shape
    tok = lambda col: pl.BlockSpec((1, tm, D_MODEL), lambda b, i: (b, i, col))
    return pl.pallas_call(
        functools.partial(_merge_ffn_kernel, tm=tm),
        grid=(bsz, seq // tm),
        in_specs=[tok(0),
                  pl.BlockSpec((1, 1, 6 * D_MODEL), lambda b, i: (b, 0, 0)),
                  tok(0), tok(0), tok(0), tok(0),
                  tok(ZC_MA // D_MODEL), tok(ZC_MB // D_MODEL),
                  _const_spec((D_MODEL, D_MODEL)), _const_spec((1, D_MODEL)),
                  _const_spec((D_MODEL, D_FF)), _const_spec((D_MODEL, D_FF)),
                  _const_spec((3, D_FF)), _const_spec((1, D_FF)), _const_spec((D_FF, D_MODEL)),
                  _const_spec((1, D_MODEL))],
        out_specs=tok(0),
        out_shape=jax.ShapeDtypeStruct((bsz, seq, D_MODEL), F32),
        scratch_shapes=[pltpu.VMEM((8, D_FF), F32)],
        compiler_params=_cparams(2),
        name="merge_ffn",
    )(x, mod3, o_cmp, o_slc, o_win, o_mla, z, z, w_o, gain, wg, wu, cw, cb, wd, fg)


def _prep_w_in(w_in):
    off = {}
    o = 0
    for name, width in (("nsa_q", 1024), ("k_cmp", 256), ("v_cmp", 256), ("k_slc", 256),
                        ("v_slc", 256), ("k_win", 256), ("v_win", 256), ("nsa_gate", 48),
                        ("mla_cq", 256), ("mla_ckv", 128), ("mla_krope", 32),
                        ("merge_a", 1024), ("merge_b", 1024)):
        off[name] = (o, o + width)
        o += width
    col = lambda name: w_in[:, off[name][0]:off[name][1]]
    q = col("nsa_q") * (NSA_D ** -0.5 * LOG2E)
    pad = jnp.zeros((D_MODEL, LANES - MLA_ROPE - 3 * NSA_HEADS), w_in.dtype)
    kv_cmp = jnp.stack([col("k_cmp").reshape(D_MODEL, NSA_GROUPS, NSA_D),
                        col("v_cmp").reshape(D_MODEL, NSA_GROUPS, NSA_D)],
                       axis=2).reshape(D_MODEL, CMP_WIDTH)
    parts = [col("merge_a"), col("merge_b"), col("k_slc"), col("v_slc"), col("k_win"),
             col("v_win"), col("mla_cq"), col("mla_ckv"), col("mla_krope"), col("nsa_gate"),
             pad, kv_cmp]
    return jnp.concatenate(parts, axis=1).astype(BF16), q.T.astype(BF16)


def _prep_mla_weights(w_uq, w_ukv):
    half = MLA_ROPE // 2
    dq = MLA_NOPE + MLA_ROPE
    wq = w_uq.reshape(MLA_Q_RANK, MLA_HEADS, dq)
    nope, x1, x2 = wq[..., :MLA_NOPE], wq[..., MLA_NOPE:MLA_NOPE + half], wq[..., MLA_NOPE + half:]
    z32 = jnp.zeros((MLA_Q_RANK, MLA_HEADS, LANES - dq), w_uq.dtype)
    wqa = jnp.concatenate([nope, x1, x2, z32], axis=-1)
    wqb = jnp.concatenate([jnp.zeros_like(nope), -x2, x1, z32], axis=-1)
    wkv = w_ukv.reshape(MLA_KV_RANK, MLA_HEADS, MLA_NOPE + MLA_V)
    wk = jnp.concatenate([wkv[..., :MLA_NOPE],
                          jnp.zeros((MLA_KV_RANK, MLA_HEADS, LANES - MLA_NOPE), w_ukv.dtype)],
                         axis=-1)
    wv = wkv[..., MLA_NOPE:]
    flat = lambda w: w.reshape(w.shape[0], MLA_HEADS * LANES).astype(BF16)
    return flat(wqa).T, flat(wqb).T, flat(wk), flat(wv).T


def _rope_constants():
    half = MLA_ROPE // 2
    pa = np.zeros((LANES, LANES), np.float32)
    pb = np.zeros((LANES, LANES), np.float32)
    for j in range(half):
        pa[j, MLA_NOPE + j] = 1.0
        pa[half + j, MLA_NOPE + half + j] = 1.0
        pb[half + j, MLA_NOPE + j] = -1.0
        pb[j, MLA_NOPE + half + j] = 1.0
    inv_freq = ROPE_THETA ** (-jnp.arange(0, MLA_ROPE, 2, dtype=F32) / MLA_ROPE)
    return jnp.asarray(pa, BF16), jnp.asarray(pb, BF16), inv_freq.reshape(half, 1)


def _overlap(seq):
    nc = (seq - CMP_LEN) // CMP_STRIDE + 1
    nb = seq // SEL_BLOCK
    cs = np.arange(nc) * CMP_STRIDE
    bs = np.arange(nb) * SEL_BLOCK
    ov = np.clip(np.minimum(cs[:, None] + CMP_LEN, bs[None, :] + SEL_BLOCK)
                 - np.maximum(cs[:, None], bs[None, :]), 0, None) / CMP_LEN
    out = np.zeros((nb, N_CMP_PAD), np.float32)
    out[:, :nc] = ov.T
    return jnp.asarray(out)


def kernel(x, c, positions, rel_bias_table, ada_w, ada_b, norm_mix_g, w_in, cmp_pos_k, cmp_w1_k, cmp_w2_k, cmp_pos_v, cmp_w1_v, cmp_w2_v, mla_q_norm_g, mla_w_uq, mla_kv_norm_g, mla_w_ukv, w_o, norm_ffn_g, ffn_w_gate, ffn_w_up, ffn_conv_w, ffn_conv_b, ffn_w_down, final_norm_g):
    bsz, seq, _ = x.shape
    assert ada_w.shape[0] == 1 and seq == N_SEL_BLOCKS * SEL_BLOCK
    n_back = WINDOW // CHUNK

    bias_c = _cmp_bias_table(rel_bias_table, seq)
    bias_s = _chunk_bias_table(rel_bias_table, limit=None, sub_far=True, n_tiles=3)
    bias_w = _chunk_bias_table(rel_bias_table, limit=WINDOW, sub_far=False, n_tiles=3)

    mod3 = _ada(c, ada_w[0], ada_b[0]).reshape(bsz, 1, 6 * D_MODEL)
    z, qt, cmp_in = _inproj(x, mod3, norm_mix_g, *_prep_w_in(w_in[0]))

    kvc, kvct = _compress(cmp_in,
                          jnp.stack([cmp_pos_k[0].reshape(1, -1), cmp_pos_v[0].reshape(1, -1)]),
                          jnp.stack([cmp_w1_k[0], cmp_w1_v[0]]),
                          jnp.stack([cmp_w2_k[0], cmp_w2_v[0]]))

    o_cmp, mb = _cmp_attn(z, qt, kvc, kvct, bias_c, _overlap(seq))
    o_slc = _nsa_flash(z, qt, mb, bias_s, k_col=ZC_KS, v_col=ZC_VS, n_back=None, gate_idx=1)
    o_win = _nsa_flash(z, qt, None, bias_w, k_col=ZC_KW, v_col=ZC_VW, n_back=n_back,
                       gate_idx=2)

    wqa, wqb, wk, wvt = _prep_mla_weights(mla_w_uq[0], mla_w_ukv[0])
    pa, pb, frq = _rope_constants()
    q_m, k_m, vt_m = _mla_prep(z, positions.reshape(bsz, 1, seq), mla_q_norm_g, mla_kv_norm_g,
                               wqa, wqb, wk, wvt, pa, pb, frq)
    o_mla = _mla_attn(q_m, k_m, vt_m)

    return _merge_ffn(x, mod3, o_cmp, o_slc, o_win, o_mla, z, w_o[0].astype(BF16), norm_ffn_g,
                      ffn_w_gate[0].astype(BF16), ffn_w_up[0].astype(BF16),
                      ffn_conv_w[0], ffn_conv_b, ffn_w_down[0].astype(BF16),
                      final_norm_g.reshape(1, D_MODEL))
```

```python
import functools
import math

import numpy as np
import jax
import jax.numpy as jnp
from jax import lax
from jax.experimental import pallas as pl
from jax.experimental.pallas import tpu as pltpu

F32 = jnp.float32
BF16 = jnp.bfloat16
HIGHEST = lax.Precision.HIGHEST

D_MODEL = 1024
NSA_HEADS = 16
NSA_GROUPS = 4
NSA_HPG = 4
NSA_D = 64
CMP_LEN = 32
CMP_STRIDE = 16
CMP_HIDDEN = 256
SEL_BLOCK = 64
SEL_TOPK = 16
WINDOW = 512
MLA_HEADS = 8
MLA_Q_RANK = 256
MLA_KV_RANK = 128
MLA_NOPE = 64
MLA_ROPE = 32
MLA_V = 128
ROPE_THETA = 10000.0
REL_BUCKETS = 32
REL_MAX_DIST = 128
D_FF = 2816
RMS_EPS = 1e-6
NEG = -1e30
LOG2E = math.log2(math.e)

LANES = 128
CHUNK = 256
CMP_TQ = 512
N_CMP_PAD = 128
N_SEL_BLOCKS = 32
ONES_ROWS = 16
FFN_COL_SPLITS = (0, 1536, D_FF)

ZC_MA = 0
ZC_MB = 1024
ZC_KS = 2048
ZC_VS = 2304
ZC_KW = 2560
ZC_VW = 2816
ZC_CQ = 3072
ZC_CKV = 3328
ZC_MISC = 3456
Z_WIDTH = 3584
CMP_WIDTH = 512
QT_ROWS = NSA_HEADS * LANES
GATE_LANE0 = MLA_ROPE
MASK_ROW0 = NSA_D

VMEM_LIMIT = 56 * 1024 * 1024

_NT = (((1,), (1,)), ((), ()))


def _cparams(n_axes):
    return pltpu.CompilerParams(
        dimension_semantics=("arbitrary",) * n_axes,
        vmem_limit_bytes=VMEM_LIMIT)


def _sigmoid(x):
    return 0.5 * jnp.tanh(0.5 * x) + 0.5


def _const_spec(shape):
    return pl.BlockSpec(shape, lambda *_: (0,) * len(shape), pipeline_mode=pl.Buffered(1))


def _t5_bucket(dist):
    n = jnp.maximum(dist, 0)
    exact = REL_BUCKETS // 2
    nf = jnp.maximum(n, exact).astype(F32)
    log_ratio = jnp.log(nf / exact) / math.log(REL_MAX_DIST / exact)
    large = jnp.minimum(exact + (log_ratio * (REL_BUCKETS - exact)).astype(jnp.int32),
                        REL_BUCKETS - 1)
    return jnp.where(n < exact, n, large)


def _bias_kernel(rel_ref, out_ref, *, rows, cols, row_coef, offset):
    m = pl.program_id(0)
    r = lax.broadcasted_iota(jnp.int32, (rows, cols), 0)
    c = lax.broadcasted_iota(jnp.int32, (rows, cols), 1)
    dist = m * cols + c + row_coef * r + offset
    bucket = _t5_bucket(dist)
    for h in range(NSA_HEADS):
        val = jnp.zeros((rows, cols), F32)
        for b in range(REL_BUCKETS):
            val = jnp.where(bucket == b, rel_ref[b, h], val)
        out_ref[h] = jnp.where(dist >= 0, val * LOG2E, NEG)


def _toeplitz_bias_kernel(rel_ref, out_ref, *, limit, sub_far):
    m = pl.program_id(0)
    span = 2 * CHUNK
    j = lax.broadcasted_iota(jnp.int32, (8, span), 1)
    dist = m * CHUNK + j - (CHUNK - 1)
    valid = dist >= 0
    if limit is not None:
        valid = valid & (dist < limit)
    bucket = _t5_bucket(dist)
    for h in range(NSA_HEADS):
        val = jnp.zeros((8, span), F32)
        for b in range(REL_BUCKETS):
            val = jnp.where(bucket == b, rel_ref[b, h], val)
        if sub_far:
            val = val - rel_ref[REL_BUCKETS - 1, h]
        val = jnp.where(valid, val * LOG2E, NEG)
        rows = jnp.broadcast_to(val[0:1, :], (CHUNK, span))
        tile = pltpu.roll(rows, span - (CHUNK - 1), 1, stride=1, stride_axis=0)[:, 0:CHUNK]
        hh = h % NSA_HPG
        out_ref[0, h // NSA_HPG, :, hh * CHUNK:(hh + 1) * CHUNK] = tile


def _cmp_bias_table(rel, seq):
    return pl.pallas_call(
        functools.partial(_bias_kernel, rows=N_CMP_PAD, cols=CMP_TQ, row_coef=-CMP_STRIDE,
                          offset=-(CMP_LEN - 1)),
        grid=(seq // CMP_TQ,),
        in_specs=[pl.BlockSpec(memory_space=pltpu.SMEM)],
        out_specs=pl.BlockSpec((NSA_HEADS, N_CMP_PAD, CMP_TQ), lambda m: (0, 0, m)),
        out_shape=jax.ShapeDtypeStruct((NSA_HEADS, N_CMP_PAD, seq), F32),
        compiler_params=_cparams(1),
        name="bias_cmp",
    )(rel)


def _chunk_bias_table(rel, *, limit, sub_far, n_tiles):
    return pl.pallas_call(
        functools.partial(_toeplitz_bias_kernel, limit=limit, sub_far=sub_far),
        grid=(n_tiles,),
        in_specs=[pl.BlockSpec(memory_space=pltpu.SMEM)],
        out_specs=pl.BlockSpec((1, NSA_GROUPS, CHUNK, NSA_HPG * CHUNK), lambda m: (m, 0, 0, 0)),
        out_shape=jax.ShapeDtypeStruct((n_tiles, NSA_GROUPS, CHUNK, NSA_HPG * CHUNK), F32),
        compiler_params=_cparams(1),
        name="bias_chunk",
    )(rel)


def _ada_kernel(c_ref, w_ref, b_ref, o_ref):
    c = c_ref[...]
    cond = c * _sigmoid(c)
    o_ref[...] = jnp.dot(cond, w_ref[...], precision=HIGHEST,
                         preferred_element_type=F32) + b_ref[...]


def _ada(c, w, b):
    bsz = c.shape[0]
    n = w.shape[1]
    tn = 1024
    return pl.pallas_call(
        _ada_kernel,
        grid=(n // tn,),
        in_specs=[pl.BlockSpec((bsz, D_MODEL), lambda j: (0, 0)),
                  pl.BlockSpec((D_MODEL, tn), lambda j: (0, j)),
                  pl.BlockSpec((1, tn), lambda j: (0, j))],
        out_specs=pl.BlockSpec((bsz, tn), lambda j: (0, j)),
        out_shape=jax.ShapeDtypeStruct((bsz, n), F32),
        compiler_params=_cparams(1),
        name="ada_mod",
    )(c, w, b.reshape(1, n))


def _modulated_norm(x, gain, shift, scale):
    ms = jnp.mean(x * x, axis=-1, keepdims=True)
    return (x * lax.rsqrt(ms + RMS_EPS) * gain) * (1.0 + scale) + shift


def _inproj_kernel(x_ref, mod_ref, g_ref, w_ref, wq_ref, z_ref, qt_ref, cmp_ref, *, tn):
    mod = mod_ref[0]
    h = _modulated_norm(x_ref[0], g_ref[...], mod[:, 0:D_MODEL],
                        mod[:, D_MODEL:2 * D_MODEL])
    hb = h.astype(BF16)
    tm = hb.shape[0]
    qt = lax.dot_general(wq_ref[...], hb, _NT, preferred_element_type=F32)
    zeros = jnp.zeros((LANES - NSA_D, tm), BF16)
    for hd in range(NSA_HEADS):
        qt_ref[0, hd * LANES:hd * LANES + NSA_D, :] = (
            qt[hd * NSA_D:(hd + 1) * NSA_D, :].astype(BF16))
        qt_ref[0, hd * LANES + NSA_D:(hd + 1) * LANES, :] = zeros
    for n0 in range(0, Z_WIDTH, tn):
        z_ref[0, :, n0:n0 + tn] = jnp.dot(
            hb, w_ref[:, n0:n0 + tn], preferred_element_type=F32).astype(BF16)
    acc = jnp.dot(hb, w_ref[:, Z_WIDTH:Z_WIDTH + CMP_WIDTH], preferred_element_type=F32)
    for g in range(NSA_GROUPS):
        cmp_ref[0, g] = acc[:, g * LANES:(g + 1) * LANES]


def _inproj(x, mod3, gain, w_p, wq_t, tm=512):
    bsz, seq, _ = x.shape
    return pl.pallas_call(
        functools.partial(_inproj_kernel, tn=512),
        grid=(bsz, seq // tm),
        in_specs=[pl.BlockSpec((1, tm, D_MODEL), lambda b, i: (b, i, 0)),
                  pl.BlockSpec((1, 1, 6 * D_MODEL), lambda b, i: (b, 0, 0)),
                  pl.BlockSpec((1, D_MODEL), lambda b, i: (0, 0)),
                  _const_spec(w_p.shape), _const_spec(wq_t.shape)],
        out_specs=[pl.BlockSpec((1, tm, Z_WIDTH), lambda b, i: (b, i, 0)),
                   pl.BlockSpec((1, QT_ROWS, tm), lambda b, i: (b, 0, i)),
                   pl.BlockSpec((1, NSA_GROUPS, tm, LANES), lambda b, i: (b, 0, i, 0))],
        out_shape=[jax.ShapeDtypeStruct((bsz, seq, Z_WIDTH), BF16),
                   jax.ShapeDtypeStruct((bsz, QT_ROWS, seq), BF16),
                   jax.ShapeDtypeStruct((bsz, NSA_GROUPS, seq, LANES), F32)],
        compiler_params=_cparams(2),
        name="in_proj",
    )(x, mod3, gain, w_p, wq_t)


def _gelu_tanh(x):
    return 0.5 * x * (1.0 + jnp.tanh(math.sqrt(2.0 / math.pi) * (x + 0.044715 * (x * x * x))))


def _pos_term_kernel(pos_ref, w1_ref, o_ref):
    pos = jnp.broadcast_to(pos_ref[0], (8, CMP_LEN * NSA_D))
    o_ref[0] = jnp.dot(pos, w1_ref[0], precision=HIGHEST, preferred_element_type=F32)


def _pos_term(pos, w1):
    return pl.pallas_call(
        _pos_term_kernel,
        grid=(2,),
        in_specs=[pl.BlockSpec((1, 1, CMP_LEN * NSA_D), lambda j: (j, 0, 0)),
                  pl.BlockSpec((1, CMP_LEN * NSA_D, CMP_HIDDEN), lambda j: (j, 0, 0))],
        out_specs=pl.BlockSpec((1, 8, CMP_HIDDEN), lambda j: (j, 0, 0)),
        out_shape=jax.ShapeDtypeStruct((2, 8, CMP_HIDDEN), F32),
        compiler_params=_cparams(1),
        name="cmp_pos_term",
    )(pos, w1)


def _compress_kernel(y_ref, pterm_ref, w1_ref, w2_ref, w2t_ref, kc_ref, vct_ref):
    nhb = y_ref.shape[2] // CMP_STRIDE
    low = lax.broadcasted_iota(jnp.int32, (nhb, LANES), 1) < NSA_D
    k_rows, v_rows = [], []
    for g in range(NSA_GROUPS):
        k_cols, v_cols = [], []
        for r in range(0, CMP_STRIDE, 2):
            a = y_ref[0, g, pl.ds(r, nhb, stride=CMP_STRIDE), :]
            b = y_ref[0, g, pl.ds(r + 1, nhb, stride=CMP_STRIDE), :]
            k_cols.append(jnp.where(low, a, pltpu.roll(b, NSA_D, 1)))
            v_cols.append(jnp.where(low, pltpu.roll(a, NSA_D, 1), b))
        k_rows.append(jnp.concatenate(k_cols, axis=1).astype(BF16))
        v_rows.append(jnp.concatenate(v_cols, axis=1).astype(BF16))
    half = (CMP_LEN // 2) * NSA_D
    outs = []
    for kv, rows in enumerate((k_rows, v_rows)):
        y = jnp.concatenate(rows, axis=0)
        top = jnp.dot(y, w1_ref[kv, 0:half, :], preferred_element_type=F32)
        bot = jnp.dot(y, w1_ref[kv, half:2 * half, :], preferred_element_type=F32)
        hidden = top + pltpu.roll(bot, NSA_GROUPS * nhb - 1, 0) + pterm_ref[kv, 0:1, :]
        outs.append(_gelu_tanh(hidden).astype(BF16))
    out = jnp.dot(outs[0], w2_ref[0], preferred_element_type=F32).astype(BF16)
    out_t = lax.dot_general(w2t_ref[1], outs[1], _NT,
                            preferred_element_type=F32).astype(BF16)
    for g in range(NSA_GROUPS):
        og = out[g * nhb:(g + 1) * nhb]
        kc_ref[0, g] = jnp.concatenate([og, jnp.zeros_like(og)], axis=1)
        vct_ref[0, g] = out_t[:, g * nhb:(g + 1) * nhb]


def _compress(cmp_in, pos, w1, w2):
    bsz, ng, seq, _ = cmp_in.shape
    nhb = seq // CMP_STRIDE
    w2b = w2.astype(BF16)
    return pl.pallas_call(
        _compress_kernel,
        grid=(bsz,),
        in_specs=[pl.BlockSpec((1, ng, seq, LANES), lambda b: (b, 0, 0, 0)),
                  _const_spec((2, 8, CMP_HIDDEN)),
                  _const_spec((2, CMP_LEN * NSA_D, CMP_HIDDEN)),
                  _const_spec((2, CMP_HIDDEN, NSA_D)),
                  _const_spec((2, NSA_D, CMP_HIDDEN))],
        out_specs=[pl.BlockSpec((1, ng, nhb, LANES), lambda b: (b, 0, 0, 0)),
                   pl.BlockSpec((1, ng, NSA_D, nhb), lambda b: (b, 0, 0, 0))],
        out_shape=[jax.ShapeDtypeStruct((bsz, ng, nhb, LANES), BF16),
                   jax.ShapeDtypeStruct((bsz, ng, NSA_D, nhb), BF16)],
        compiler_params=_cparams(1),
        name="compress",
    )(cmp_in, _pos_term(pos, w1), w1.astype(BF16), w2b, w2b.transpose(0, 2, 1))


def _select_mask(score, jj, cur):
    forced = (jj == 0) | (jj == cur) | (jj == cur - 1)
    x = jnp.where(forced, jnp.inf, jnp.where(jj > cur, -jnp.inf, score))
    below = pltpu.bitcast(pltpu.bitcast(x, jnp.int32) - 1, F32)
    x_lo = jnp.where(x > 0.0, below, jnp.where(x == 0.0, -1.0, x))
    tq = x.shape[1]
    n_sub = 8
    masks = []
    for v in range(N_SEL_BLOCKS // n_sub):
        rows = slice(v * n_sub, (v + 1) * n_sub)
        xv, xlv = x[rows], x_lo[rows]
        jv = v * n_sub + lax.broadcasted_iota(jnp.int32, (n_sub, tq), 0)
        rank = jnp.zeros((n_sub, tq), F32)
        for j2 in range(N_SEL_BLOCKS):
            row = x[j2:j2 + 1, :]
            if v * n_sub > j2:
                thr = xlv
            elif (v + 1) * n_sub - 1 <= j2:
                thr = xv
            else:
                thr = jnp.where(jv > j2, xlv, xv)
            rank = rank + jnp.where(row > thr, 1.0, 0.0)
        masks.append(jnp.where(rank < SEL_TOPK, 0.0, NEG))
    return jnp.concatenate(masks, axis=0)


def _cmp_attn_kernel(q_ref, misc_ref, kc_ref, vct_ref, bias_ref, ovl_ref, o_ref, mb_ref,
                     gt_scr, ot_scr):
    i = pl.program_id(1)
    tq = CMP_TQ
    gt_scr[...] = _sigmoid(misc_ref[0].astype(F32)).T
    jj = lax.broadcasted_iota(jnp.int32, (N_SEL_BLOCKS, tq), 0)
    tt = i * tq + lax.broadcasted_iota(jnp.int32, (N_SEL_BLOCKS, tq), 1)
    cur = tt // SEL_BLOCK
    t_row = i * tq + lax.broadcasted_iota(jnp.int32, (1, tq), 1)
    sees_any = t_row >= CMP_LEN - 1
    logits = [jnp.dot(kc_ref[0, h // NSA_HPG], q_ref[0, h * LANES:(h + 1) * LANES, :],
                      preferred_element_type=F32) + bias_ref[h]
              for h in range(NSA_HEADS)]
    for g in range(NSA_GROUPS):
        vct = vct_ref[0, g]
        psum = jnp.zeros((N_CMP_PAD, tq), F32)
        for hh in range(NSA_HPG):
            h = g * NSA_HPG + hh
            s = logits[h]
            e = jnp.exp2(s - jnp.max(s, axis=0, keepdims=True))
            inv = jnp.where(sees_any, 1.0 / jnp.sum(e, axis=0, keepdims=True), 0.0)
            p = e * inv
            psum = psum + p
            o = jnp.dot(vct, p.astype(BF16), preferred_element_type=F32)
            lane = GATE_LANE0 + 3 * h
            ot_scr[h * NSA_D:(h + 1) * NSA_D, :] = o * gt_scr[lane:lane + 1, :]
        score = jnp.dot(ovl_ref[...], psum, precision=HIGHEST,
                        preferred_element_type=F32)
        mb_ref[0, g * LANES:(g + 1) * LANES, :] = jnp.concatenate(
            [jnp.zeros((MASK_ROW0, tq), F32), _select_mask(score, jj, cur),
             jnp.zeros((LANES - MASK_ROW0 - N_SEL_BLOCKS, tq), F32)], axis=0).astype(BF16)
    o_ref[0] = ot_scr[...].T.astype(BF16)


def _cmp_attn(z, qt, kc, vct, bias_c, ovl):
    bsz, seq, _ = z.shape
    tq = CMP_TQ
    return pl.pallas_call(
        _cmp_attn_kernel,
        grid=(bsz, seq // tq),
        in_specs=[pl.BlockSpec((1, QT_ROWS, tq), lambda b, i: (b, 0, i)),
                  pl.BlockSpec((1, tq, LANES), lambda b, i: (b, i, ZC_MISC // LANES)),
                  pl.BlockSpec((1, NSA_GROUPS, N_CMP_PAD, LANES), lambda b, i: (b, 0, 0, 0)),
                  pl.BlockSpec((1, NSA_GROUPS, NSA_D, N_CMP_PAD), lambda b, i: (b, 0, 0, 0)),
                  pl.BlockSpec((NSA_HEADS, N_CMP_PAD, tq), lambda b, i: (0, 0, i)),
                  pl.BlockSpec((N_SEL_BLOCKS, N_CMP_PAD), lambda b, i: (0, 0))],
        out_specs=[pl.BlockSpec((1, tq, D_MODEL), lambda b, i: (b, i, 0)),
                   pl.BlockSpec((1, NSA_GROUPS * LANES, tq), lambda b, i: (b, 0, i))],
        out_shape=[jax.ShapeDtypeStruct((bsz, seq, D_MODEL), BF16),
                   jax.ShapeDtypeStruct((bsz, NSA_GROUPS * LANES, seq), BF16)],
        scratch_shapes=[pltpu.VMEM((LANES, tq), F32),
                        pltpu.VMEM((D_MODEL, tq), F32)],
        compiler_params=_cparams(2),
        name="cmp_attn_select",
    )(qt, z, kc, vct, bias_c, ovl)


def _flash_produce(s_ref, mx_ref, s):
    s_ref[...] = s
    mx_ref[...] = jnp.max(s, axis=0, keepdims=True)


def _flash_consume(s_ref, mx_ref, vt, m_ref, acc_ref):
    m_old = m_ref[...]
    m_new = jnp.maximum(m_old, mx_ref[...])
    alpha = jnp.exp2(m_old - m_new)
    p = jnp.exp2((s_ref[...] - m_new).astype(BF16))
    acc_ref[...] = alpha * acc_ref[...] + jnp.dot(vt, p, preferred_element_type=F32)
    m_ref[...] = m_new


def _flash_reset(m_ref, acc_ref):
    m_ref[...] = jnp.full(m_ref.shape, NEG, F32)
    acc_ref[...] = jnp.zeros(acc_ref.shape, F32)


def _ones_rows(n_cols):
    row = lax.broadcasted_iota(jnp.int32, (ONES_ROWS, n_cols), 0)
    return jnp.where(row == 0, 1.0, 0.0).astype(BF16)


def _run_flash(lo, hi, streams, produce_one, consume_one, common_n=None):
    n = hi - lo + 1

    def produce(c, slot, kind):
        for s in streams:
            produce_one(s, c, slot, kind)

    def consume(c, slot):
        for s in streams:
            consume_one(s, c, slot)

    def both(cp, sp, kind, cc, sc):
        for s in streams:
            produce_one(s, cp, sp, kind)
            consume_one(s, cc, sc)

    def generic():
        produce(lo, 0, "first")

        @pl.when(n % 2 == 1)
        def _():
            nxt = jnp.minimum(lo + 1, hi)
            for s in streams:
                consume_one(s, lo, 0)
                produce_one(s, nxt, 0, "first")

        start = lo + n % 2
        n_pairs = n // 2

        def pair(t, carry):
            c = start + 2 * t
            both(c + 1, 1, "far", c, 0)
            both(c + 2, 0, "trail", c + 1, 1)
            return carry

        lax.fori_loop(0, n_pairs - 1, pair, 0)

        @pl.when(n_pairs >= 1)
        def _():
            both(hi, 1, "last", hi - 1, 0)
            consume(hi, 1)

    if common_n is None:
        generic()
        return

    @pl.when(n == common_n)
    def _():
        produce(lo, 0, "first")
        for k in range(common_n - 1):
            both(lo + k + 1, (k + 1) % 2, "last" if k + 2 == common_n else "first",
                 lo + k, k % 2)
        consume(hi, (common_n - 1) % 2)

    @pl.when(n != common_n)
    def _():
        generic()


def _nsa_flash_kernel(*refs, use_mask, n_back, gate_idx):
    if use_mask:
        q_ref, mb_ref, *refs = refs
    else:
        q_ref, *refs = refs
    misc_ref, k_ref, v_ref, bias_ref, o_ref, kaug, vt, qs_scr, gt_scr, ot_scr, *flash_scr = refs
    i = pl.program_id(1)
    seq = k_ref.shape[1]

    @pl.when(i == 0)
    def _():
        if use_mask:
            blk = lax.broadcasted_iota(jnp.int32, (seq, NSA_D), 0) // SEL_BLOCK
            lane = lax.broadcasted_iota(jnp.int32, (seq, NSA_D), 1)
            extra = jnp.where(blk == lane, 1.0, 0.0).astype(BF16)
        else:
            extra = jnp.zeros((seq, NSA_D), BF16)
        eye = (lax.broadcasted_iota(jnp.int32, (NSA_D, NSA_D), 0)
               == lax.broadcasted_iota(jnp.int32, (NSA_D, NSA_D), 1)).astype(BF16)
        for g in range(NSA_GROUPS):
            gs = slice(g * NSA_D, (g + 1) * NSA_D)
            kaug[g] = jnp.concatenate([k_ref[0, :, gs], extra], axis=1)
            for c in range(seq // CHUNK):
                vt[g, c, 0:NSA_D, :] = lax.dot_general(
                    eye, v_ref[0, c * CHUNK:(c + 1) * CHUNK, gs], _NT,
                    preferred_element_type=F32).astype(BF16)
                vt[g, c, NSA_D:NSA_D + ONES_ROWS, :] = _ones_rows(CHUNK)

    gt_scr[...] = _sigmoid(misc_ref[0].astype(F32)).T
    for h in range(NSA_HEADS):
        g, hh = divmod(h, NSA_HPG)
        qh = q_ref[0, h * LANES:(h + 1) * LANES, :]
        if use_mask:
            qh = qh + mb_ref[0, g * LANES:(g + 1) * LANES, :]
        qs_scr[g, :, hh * CHUNK:(hh + 1) * CHUNK] = qh

    s_scr, mx_scr, m_scr, acc_scr = flash_scr
    heads = range(NSA_HEADS)

    def view(ref, h, *lead):
        g, hh = divmod(h, NSA_HPG)
        return ref.at[(*lead, g, slice(None), pl.ds(hh * CHUNK, CHUNK))]

    for h in heads:
        _flash_reset(view(m_scr, h), view(acc_scr, h))

    def produce(h, c, slot, kind):
        g = h // NSA_HPG
        off = pl.multiple_of(c * CHUNK, CHUNK)
        s = jnp.dot(kaug[g, pl.ds(off, CHUNK), :], view(qs_scr, h)[...],
                    preferred_element_type=F32)
        if kind == "last":
            s = s + view(bias_ref, h, 0)[...]
        elif kind != "far" or n_back is not None:
            s = s + view(bias_ref, h, jnp.minimum(i - c, 2))[...]
        _flash_produce(view(s_scr, h, slot), view(mx_scr, h, slot), s)

    def consume(h, c, slot):
        _flash_consume(view(s_scr, h, slot), view(mx_scr, h, slot), vt[h // NSA_HPG, c],
                       view(m_scr, h), view(acc_scr, h))

    if n_back is None:
        _run_flash(0, i, heads, produce, consume)
    else:
        _run_flash(jnp.maximum(i - n_back, 0), i, heads, produce, consume,
                   common_n=n_back + 1)
    for h in heads:
        acc = view(acc_scr, h)
        lane = GATE_LANE0 + 3 * h + gate_idx
        scale = gt_scr[lane:lane + 1, :] / acc[NSA_D:NSA_D + 1, :]
        ot_scr[h * NSA_D:(h + 1) * NSA_D, :] = acc[0:NSA_D, :] * scale
    o_ref[0] = ot_scr[...].T.astype(BF16)


def _nsa_flash(z, qt, mb, bias_t, *, k_col, v_col, n_back, gate_idx):
    bsz, seq, _ = z.shape
    use_mask = mb is not None
    wide = NSA_HPG * CHUNK
    in_specs = [pl.BlockSpec((1, QT_ROWS, CHUNK), lambda b, i: (b, 0, i))]
    args = [qt]
    if use_mask:
        in_specs.append(pl.BlockSpec((1, NSA_GROUPS * LANES, CHUNK), lambda b, i: (b, 0, i)))
        args.append(mb)
    in_specs += [pl.BlockSpec((1, CHUNK, LANES), lambda b, i: (b, i, ZC_MISC // LANES)),
                 pl.BlockSpec((1, seq, 256), lambda b, i: (b, 0, k_col // 256)),
                 pl.BlockSpec((1, seq, 256), lambda b, i: (b, 0, v_col // 256)),
                 _const_spec(bias_t.shape)]
    args += [z, z, z, bias_t]
    scratch = [pltpu.VMEM((NSA_GROUPS, seq, LANES), BF16),
               pltpu.VMEM((NSA_GROUPS, seq // CHUNK, NSA_D + ONES_ROWS, CHUNK), BF16),
               pltpu.VMEM((NSA_GROUPS, LANES, wide), BF16),
               pltpu.VMEM((LANES, CHUNK), F32),
               pltpu.VMEM((D_MODEL, CHUNK), F32),
               pltpu.VMEM((2, NSA_GROUPS, CHUNK, wide), F32),
               pltpu.VMEM((2, NSA_GROUPS, 1, wide), F32),
               pltpu.VMEM((NSA_GROUPS, 1, wide), F32),
               pltpu.VMEM((NSA_GROUPS, NSA_D + ONES_ROWS, wide), F32)]
    return pl.pallas_call(
        functools.partial(_nsa_flash_kernel, use_mask=use_mask, n_back=n_back,
                          gate_idx=gate_idx),
        grid=(bsz, seq // CHUNK),
        in_specs=in_specs,
        out_specs=pl.BlockSpec((1, CHUNK, D_MODEL), lambda b, i: (b, i, 0)),
        out_shape=jax.ShapeDtypeStruct((bsz, seq, D_MODEL), BF16),
        scratch_shapes=scratch,
        compiler_params=_cparams(2),
        name="slc_attn" if use_mask else "win_attn",
    )(*args)


def _rms(x, gain):
    ms = jnp.mean(x * x, axis=-1, keepdims=True)
    return x * lax.rsqrt(ms + RMS_EPS) * gain


def _mla_prep_kernel(cq_ref, ckv_ref, misc_ref, pos_ref, qg_ref, kvg_ref, wqa_ref, wqb_ref,
                     wk_ref, wvt_ref, pa_ref, pb_ref, frq_ref, q_out, k_out, vt_out):
    tm = cq_ref.shape[1]
    cqn = _rms(cq_ref[0].astype(F32), qg_ref[...]).astype(BF16)
    ckvn = _rms(ckv_ref[0].astype(F32), kvg_ref[...]).astype(BF16)
    half = MLA_ROPE // 2
    ang = frq_ref[...] * pos_ref[0].astype(F32)
    cos_t, sin_t = jnp.cos(ang), jnp.sin(ang)
    ones = jnp.ones((MLA_NOPE, tm), F32)
    zeros = jnp.zeros((LANES - MLA_NOPE - 2 * half, tm), F32)
    cos_rows = jnp.concatenate([ones, cos_t, cos_t, zeros], axis=0)
    sin_rows = jnp.concatenate([0.0 * ones, sin_t, sin_t, zeros], axis=0)
    cos, sin = cos_rows.T, sin_rows.T
    scale = (MLA_NOPE + MLA_ROPE) ** -0.5 * LOG2E
    qa = lax.dot_general(wqa_ref[...], cqn, _NT, preferred_element_type=F32)
    qb = lax.dot_general(wqb_ref[...], cqn, _NT, preferred_element_type=F32)
    misc = misc_ref[0]
    kr = (jnp.dot(misc, pa_ref[...], preferred_element_type=F32) * cos
          + jnp.dot(misc, pb_ref[...], preferred_element_type=F32) * sin)
    kn = jnp.dot(ckvn, wk_ref[...], preferred_element_type=F32)
    vt = lax.dot_general(wvt_ref[...], ckvn, _NT,
                         preferred_element_type=F32).astype(BF16)
    for h in range(MLA_HEADS):
        sl = slice(h * LANES, (h + 1) * LANES)
        q_out[0, h] = ((qa[sl, :] * cos_rows + qb[sl, :] * sin_rows) * scale).astype(BF16)
        k_out[0, h] = (kn[:, sl] + kr).astype(BF16)
        for cc in range(tm // CHUNK):
            vt_out[0, h, cc, 0:MLA_V, :] = vt[sl, cc * CHUNK:(cc + 1) * CHUNK]
            vt_out[0, h, cc, MLA_V:MLA_V + ONES_ROWS, :] = _ones_rows(CHUNK)


def _mla_prep(z, pos3, qg, kvg, wqa, wqb, wk, wvt, pa, pb, frq, tm=1024):
    bsz, seq, _ = z.shape
    full = lambda shape: pl.BlockSpec(shape, lambda b, i: (0,) * len(shape))
    qk_shape = jax.ShapeDtypeStruct((bsz, MLA_HEADS, seq, LANES), BF16)
    qk_spec = pl.BlockSpec((1, MLA_HEADS, tm, LANES), lambda b, i: (b, 0, i, 0))
    qt_shape = jax.ShapeDtypeStruct((bsz, MLA_HEADS, LANES, seq), BF16)
    qt_spec = pl.BlockSpec((1, MLA_HEADS, LANES, tm), lambda b, i: (b, 0, 0, i))
    return pl.pallas_call(
        _mla_prep_kernel,
        grid=(bsz, seq // tm),
        in_specs=[pl.BlockSpec((1, tm, MLA_Q_RANK), lambda b, i: (b, i, ZC_CQ // MLA_Q_RANK)),
                  pl.BlockSpec((1, tm, MLA_KV_RANK), lambda b, i: (b, i, ZC_CKV // MLA_KV_RANK)),
                  pl.BlockSpec((1, tm, LANES), lambda b, i: (b, i, ZC_MISC // LANES)),
                  pl.BlockSpec((1, 1, tm), lambda b, i: (b, 0, i)),
                  full((1, MLA_Q_RANK)), full((1, MLA_KV_RANK)),
                  full((MLA_HEADS * LANES, MLA_Q_RANK)), full((MLA_HEADS * LANES, MLA_Q_RANK)),
                  full((MLA_KV_RANK, MLA_HEADS * LANES)), full((MLA_HEADS * MLA_V, MLA_KV_RANK)),
                  full((LANES, LANES)), full((LANES, LANES)), full((MLA_ROPE // 2, 1))],
        out_specs=[qt_spec, qk_spec,
                   pl.BlockSpec((1, MLA_HEADS, tm // CHUNK, MLA_V + ONES_ROWS, CHUNK),
                                lambda b, i: (b, 0, i, 0, 0))],
        out_shape=[qt_shape, qk_shape,
                   jax.ShapeDtypeStruct((bsz, MLA_HEADS, seq // CHUNK, MLA_V + ONES_ROWS, CHUNK),
                                        BF16)],
        compiler_params=_cparams(2),
        name="mla_prep",
    )(z, z, z, pos3, qg, kvg, wqa, wqb, wk, wvt, pa, pb, frq)


def _mla_attn_kernel(q_ref, k_ref, vt_ref, o_ref, ot_scr, s_scr, mx_scr, m_scr, acc_scr):
    i = pl.program_id(1)
    causal = (lax.broadcasted_iota(jnp.int32, (CHUNK, CHUNK), 0)
              <= lax.broadcasted_iota(jnp.int32, (CHUNK, CHUNK), 1))

    heads = range(MLA_HEADS)
    for h in heads:
        _flash_reset(m_scr.at[h], acc_scr.at[h])

    def produce(h, c, slot, kind):
        off = pl.multiple_of(c * CHUNK, CHUNK)
        s = jnp.dot(k_ref[0, h, pl.ds(off, CHUNK), :], q_ref[0, h],
                    preferred_element_type=F32)
        if kind == "last":
            s = jnp.where(causal, s, NEG)
        elif kind == "first":
            s = jnp.where(jnp.logical_or(causal, c < i), s, NEG)
        _flash_produce(s_scr.at[slot, h], mx_scr.at[slot, h], s)

    def consume(h, c, slot):
        _flash_consume(s_scr.at[slot, h], mx_scr.at[slot, h], vt_ref[0, h, c],
                       m_scr.at[h], acc_scr.at[h])

    _run_flash(0, i, heads, produce, consume)
    for h in heads:
        ot_scr[h * MLA_V:(h + 1) * MLA_V, :] = (
            acc_scr[h, 0:MLA_V, :] / acc_scr[h, MLA_V:MLA_V + 1, :])
    o_ref[0] = ot_scr[...].T.astype(BF16)


def _mla_attn(qt, k, vt):
    bsz, nh, seq, _ = k.shape
    hpt = nh
    return pl.pallas_call(
        _mla_attn_kernel,
        grid=(bsz, seq // CHUNK),
        in_specs=[pl.BlockSpec((1, nh, LANES, CHUNK), lambda b, i: (b, 0, 0, i)),
                  pl.BlockSpec((1, nh, seq, LANES), lambda b, i: (b, 0, 0, 0)),
                  pl.BlockSpec((1, nh, seq // CHUNK, MLA_V + ONES_ROWS, CHUNK),
                               lambda b, i: (b, 0, 0, 0, 0))],
        out_specs=pl.BlockSpec((1, CHUNK, nh * MLA_V), lambda b, i: (b, i, 0)),
        out_shape=jax.ShapeDtypeStruct((bsz, seq, nh * MLA_V), BF16),
        scratch_shapes=[pltpu.VMEM((nh * MLA_V, CHUNK), F32),
                        pltpu.VMEM((2, hpt, CHUNK, CHUNK), F32),
                        pltpu.VMEM((2, hpt, 1, CHUNK), F32),
                        pltpu.VMEM((hpt, 1, CHUNK), F32),
                        pltpu.VMEM((hpt, MLA_V + ONES_ROWS, CHUNK), F32)],
        compiler_params=_cparams(2),
        name="mla_attn",
    )(qt, k, vt)


def _merge_ffn_kernel(x_ref, mod_ref, oc_ref, os_ref, ow_ref, om_ref, ma_ref, mb_ref, wo_ref,
                      g_ref, wg_ref, wu_ref, cw_ref, cb_ref, wd_ref, fg_ref,
                      o_ref, prev_ref, *, tm):
    i = pl.program_id(1)

    @pl.when(i == 0)
    def _():
        prev_ref[...] = jnp.zeros_like(prev_ref)

    mod = mod_ref[0]
    o_nsa = oc_ref[0].astype(F32) + os_ref[0].astype(F32) + ow_ref[0].astype(F32)
    y = (_sigmoid(ma_ref[0].astype(F32)) * o_nsa
         + _sigmoid(mb_ref[0].astype(F32)) * om_ref[0].astype(F32))
    a = jnp.dot(y.astype(BF16), wo_ref[...], preferred_element_type=F32)
    x1 = x_ref[0] + mod[:, 2 * D_MODEL:3 * D_MODEL] * a
    h2 = _modulated_norm(x1, g_ref[...], mod[:, 3 * D_MODEL:4 * D_MODEL],
                         mod[:, 4 * D_MODEL:5 * D_MODEL]).astype(BF16)
    chunks = [slice(f0, f1) for f0, f1 in zip(FFN_COL_SPLITS[:-1], FFN_COL_SPLITS[1:])]
    acc = jnp.zeros((tm, D_MODEL), F32)

    def finish(fs, gt, up, acc):
        row = lax.broadcasted_iota(jnp.int32, gt.shape, 0)
        p1 = prev_ref[7:8, fs]
        p2 = prev_ref[6:7, fs]
        g1 = jnp.where(row == 0, p1, pltpu.roll(gt, 1, 0))
        g2 = jnp.where(row == 0, p2, jnp.where(row == 1, p1, pltpu.roll(gt, 2, 0)))
        prev_ref[:, fs] = gt[tm - 8:tm, :]
        conv = cb_ref[:, fs] + cw_ref[0:1, fs] * g2 + cw_ref[1:2, fs] * g1 + cw_ref[2:3, fs] * gt
        act = (conv * _sigmoid(conv)) * up
        return acc + jnp.dot(act.astype(BF16), wd_ref[fs, :], preferred_element_type=F32)

    pending = None
    for fs in chunks:
        gt = jnp.dot(h2, wg_ref[:, fs], preferred_element_type=F32)
        up = jnp.dot(h2, wu_ref[:, fs], preferred_element_type=F32)
        if pending is not None:
            acc = finish(*pending, acc)
        pending = (fs, gt, up)
    acc = finish(*pending, acc)
    x2 = x1 + mod[:, 5 * D_MODEL:6 * D_MODEL] * acc
    ms = jnp.mean(x2 * x2, axis=-1, keepdims=True)
    o_ref[0] = x2 * lax.rsqrt(ms + RMS_EPS) * fg_ref[...]


def _merge_ffn(x, mod3, o_cmp, o_slc, o_win, o_mla, z, w_o, gain, wg, wu, cw, cb, wd, fg,
               tm=512):
    bsz, seq, _ = x.shape
    tok = lambda col: pl.BlockSpec((1, tm, D_MODEL), lambda b, i: (b, i, col))
    return pl.pallas_call(
        functools.partial(_merge_ffn_kernel, tm=tm),
        grid=(bsz, seq // tm),
        in_specs=[tok(0),
                  pl.BlockSpec((1, 1, 6 * D_MODEL), lambda b, i: (b, 0, 0)),
                  tok(0), tok(0), tok(0), tok(0),
                  tok(ZC_MA // D_MODEL), tok(ZC_MB // D_MODEL),
                  _const_spec((D_MODEL, D_MODEL)), _const_spec((1, D_MODEL)),
                  _const_spec((D_MODEL, D_FF)), _const_spec((D_MODEL, D_FF)),
                  _const_spec((3, D_FF)), _const_spec((1, D_FF)), _const_spec((D_FF, D_MODEL)),
                  _const_spec((1, D_MODEL))],
        out_specs=tok(0),
        out_shape=jax.ShapeDtypeStruct((bsz, seq, D_MODEL), F32),
        scratch_shapes=[pltpu.VMEM((8, D_FF), F32)],
        compiler_params=_cparams(2),
        name="merge_ffn",
    )(x, mod3, o_cmp, o_slc, o_win, o_mla, z, z, w_o, gain, wg, wu, cw, cb, wd, fg)


def _prep_w_in(w_in):
    off = {}
    o = 0
    for name, width in (("nsa_q", 1024), ("k_cmp", 256), ("v_cmp", 256), ("k_slc", 256),
                        ("v_slc", 256), ("k_win", 256), ("v_win", 256), ("nsa_gate", 48),
                        ("mla_cq", 256), ("mla_ckv", 128), ("mla_krope", 32),
                        ("merge_a", 1024), ("merge_b", 1024)):
        off[name] = (o, o + width)
        o += width
    col = lambda name: w_in[:, off[name][0]:off[name][1]]
    q = col("nsa_q") * (NSA_D ** -0.5 * LOG2E)
    pad = jnp.zeros((D_MODEL, LANES - MLA_ROPE - 3 * NSA_HEADS), w_in.dtype)
    kv_cmp = jnp.stack([col("k_cmp").reshape(D_MODEL, NSA_GROUPS, NSA_D),
                        col("v_cmp").reshape(D_MODEL, NSA_GROUPS, NSA_D)],
                       axis=2).reshape(D_MODEL, CMP_WIDTH)
    parts = [col("merge_a"), col("merge_b"), col("k_slc"), col("v_slc"), col("k_win"),
             col("v_win"), col("mla_cq"), col("mla_ckv"), col("mla_krope"), col("nsa_gate"),
             pad, kv_cmp]
    return jnp.concatenate(parts, axis=1).astype(BF16), q.T.astype(BF16)


def _prep_mla_weights(w_uq, w_ukv):
    half = MLA_ROPE // 2
    dq = MLA_NOPE + MLA_ROPE
    wq = w_uq.reshape(MLA_Q_RANK, MLA_HEADS, dq)
    nope, x1, x2 = wq[..., :MLA_NOPE], wq[..., MLA_NOPE:MLA_NOPE + half], wq[..., MLA_NOPE + half:]
    z32 = jnp.zeros((MLA_Q_RANK, MLA_HEADS, LANES - dq), w_uq.dtype)
    wqa = jnp.concatenate([nope, x1, x2, z32], axis=-1)
    wqb = jnp.concatenate([jnp.zeros_like(nope), -x2, x1, z32], axis=-1)
    wkv = w_ukv.reshape(MLA_KV_RANK, MLA_HEADS, MLA_NOPE + MLA_V)
    wk = jnp.concatenate([wkv[..., :MLA_NOPE],
                          jnp.zeros((MLA_KV_RANK, MLA_HEADS, LANES - MLA_NOPE), w_ukv.dtype)],
                         axis=-1)
    wv = wkv[..., MLA_NOPE:]
    flat = lambda w: w.reshape(w.shape[0], MLA_HEADS * LANES).astype(BF16)
    return flat(wqa).T, flat(wqb).T, flat(wk), flat(wv).T


def _rope_constants():
    half = MLA_ROPE // 2
    pa = np.zeros((LANES, LANES), np.float32)
    pb = np.zeros((LANES, LANES), np.float32)
    for j in range(half):
        pa[j, MLA_NOPE + j] = 1.0
        pa[half + j, MLA_NOPE + half + j] = 1.0
        pb[half + j, MLA_NOPE + j] = -1.0
        pb[j, MLA_NOPE + half + j] = 1.0
    inv_freq = ROPE_THETA ** (-jnp.arange(0, MLA_ROPE, 2, dtype=F32) / MLA_ROPE)
    return jnp.asarray(pa, BF16), jnp.asarray(pb, BF16), inv_freq.reshape(half, 1)


def _overlap(seq):
    nc = (seq - CMP_LEN) // CMP_STRIDE + 1
    nb = seq // SEL_BLOCK
    cs = np.arange(nc) * CMP_STRIDE
    bs = np.arange(nb) * SEL_BLOCK
    ov = np.clip(np.minimum(cs[:, None] + CMP_LEN, bs[None, :] + SEL_BLOCK)
                 - np.maximum(cs[:, None], bs[None, :]), 0, None) / CMP_LEN
    out = np.zeros((nb, N_CMP_PAD), np.float32)
    out[:, :nc] = ov.T
    return jnp.asarray(out)


def kernel(x, c, positions, rel_bias_table, ada_w, ada_b, norm_mix_g, w_in, cmp_pos_k, cmp_w1_k, cmp_w2_k, cmp_pos_v, cmp_w1_v, cmp_w2_v, mla_q_norm_g, mla_w_uq, mla_kv_norm_g, mla_w_ukv, w_o, norm_ffn_g, ffn_w_gate, ffn_w_up, ffn_conv_w, ffn_conv_b, ffn_w_down, final_norm_g):
    bsz, seq, _ = x.shape
    assert ada_w.shape[0] == 1 and seq == N_SEL_BLOCKS * SEL_BLOCK
    n_back = WINDOW // CHUNK

    bias_c = _cmp_bias_table(rel_bias_table, seq)
    bias_s = _chunk_bias_table(rel_bias_table, limit=None, sub_far=True, n_tiles=3)
    bias_w = _chunk_bias_table(rel_bias_table, limit=WINDOW, sub_far=False, n_tiles=3)

    mod3 = _ada(c, ada_w[0], ada_b[0]).reshape(bsz, 1, 6 * D_MODEL)
    z, qt, cmp_in = _inproj(x, mod3, norm_mix_g, *_prep_w_in(w_in[0]))

    kvc, kvct = _compress(cmp_in,
                          jnp.stack([cmp_pos_k[0].reshape(1, -1), cmp_pos_v[0].reshape(1, -1)]),
                          jnp.stack([cmp_w1_k[0], cmp_w1_v[0]]),
                          jnp.stack([cmp_w2_k[0], cmp_w2_v[0]]))

    o_cmp, mb = _cmp_attn(z, qt, kvc, kvct, bias_c, _overlap(seq))
    o_slc = _nsa_flash(z, qt, mb, bias_s, k_col=ZC_KS, v_col=ZC_VS, n_back=None, gate_idx=1)
    o_win = _nsa_flash(z, qt, None, bias_w, k_col=ZC_KW, v_col=ZC_VW, n_back=n_back,
                       gate_idx=2)

    wqa, wqb, wk, wvt = _prep_mla_weights(mla_w_uq[0], mla_w_ukv[0])
    pa, pb, frq = _rope_constants()
    q_m, k_m, vt_m = _mla_prep(z, positions.reshape(bsz, 1, seq), mla_q_norm_g, mla_kv_norm_g,
                               wqa, wqb, wk, wvt, pa, pb, frq)
    o_mla = _mla_attn(q_m, k_m, vt_m)

    return _merge_ffn(x, mod3, o_cmp, o_slc, o_win, o_mla, z, w_o[0].astype(BF16), norm_ffn_g,
                      ffn_w_gate[0].astype(BF16), ffn_w_up[0].astype(BF16),
                      ffn_conv_w[0], ffn_conv_b, ffn_w_down[0].astype(BF16),
                      final_norm_g.reshape(1, D_MODEL))
```

```python
import functools
import math

import numpy as np
import jax
import jax.numpy as jnp
from jax import lax
from jax.experimental import pallas as pl
from jax.experimental.pallas import tpu as pltpu

F32 = jnp.float32
BF16 = jnp.bfloat16
HIGHEST = lax.Precision.HIGHEST

D_MODEL = 1024
NSA_HEADS = 16
NSA_GROUPS = 4
NSA_HPG = 4
NSA_D = 64
CMP_LEN = 32
CMP_STRIDE = 16
CMP_HIDDEN = 256
SEL_BLOCK = 64
SEL_TOPK = 16
WINDOW = 512
MLA_HEADS = 8
MLA_Q_RANK = 256
MLA_KV_RANK = 128
MLA_NOPE = 64
MLA_ROPE = 32
MLA_V = 128
ROPE_THETA = 10000.0
REL_BUCKETS = 32
REL_MAX_DIST = 128
D_FF = 2816
RMS_EPS = 1e-6
NEG = -1e30
LOG2E = math.log2(math.e)

LANES = 128
CHUNK = 256
CMP_TQ = 512
N_CMP_PAD = 128
N_SEL_BLOCKS = 32
ONES_ROWS = 16
FFN_COL_SPLITS = (0, 1536, D_FF)

ZC_MA = 0
ZC_MB = 1024
ZC_KS = 2048
ZC_VS = 2304
ZC_KW = 2560
ZC_VW = 2816
ZC_CQ = 3072
ZC_CKV = 3328
ZC_MISC = 3456
Z_WIDTH = 3584
CMP_WIDTH = 512
QT_ROWS = NSA_HEADS * LANES
GATE_LANE0 = MLA_ROPE
MASK_ROW0 = NSA_D

VMEM_LIMIT = 56 * 1024 * 1024

_NT = (((1,), (1,)), ((), ()))


def _cparams(n_axes):
    return pltpu.CompilerParams(
        dimension_semantics=("arbitrary",) * n_axes,
        vmem_limit_bytes=VMEM_LIMIT)


def _sigmoid(x):
    return 0.5 * jnp.tanh(0.5 * x) + 0.5


def _const_spec(shape):
    return pl.BlockSpec(shape, lambda *_: (0,) * len(shape), pipeline_mode=pl.Buffered(1))


def _t5_bucket(dist):
    n = jnp.maximum(dist, 0)
    exact = REL_BUCKETS // 2
    nf = jnp.maximum(n, exact).astype(F32)
    log_ratio = jnp.log(nf / exact) / math.log(REL_MAX_DIST / exact)
    large = jnp.minimum(exact + (log_ratio * (REL_BUCKETS - exact)).astype(jnp.int32),
                        REL_BUCKETS - 1)
    return jnp.where(n < exact, n, large)


def _bias_kernel(rel_ref, out_ref, *, rows, cols, row_coef, offset):
    m = pl.program_id(0)
    r = lax.broadcasted_iota(jnp.int32, (rows, cols), 0)
    c = lax.broadcasted_iota(jnp.int32, (rows, cols), 1)
    dist = m * cols + c + row_coef * r + offset
    bucket = _t5_bucket(dist)
    for h in range(NSA_HEADS):
        val = jnp.zeros((rows, cols), F32)
        for b in range(REL_BUCKETS):
            val = jnp.where(bucket == b, rel_ref[b, h], val)
        out_ref[h] = jnp.where(dist >= 0, val * LOG2E, NEG)


def _toeplitz_bias_kernel(rel_ref, out_ref, *, limit, sub_far):
    m = pl.program_id(0)
    span = 2 * CHUNK
    j = lax.broadcasted_iota(jnp.int32, (8, span), 1)
    dist = m * CHUNK + j - (CHUNK - 1)
    valid = dist >= 0
    if limit is not None:
        valid = valid & (dist < limit)
    bucket = _t5_bucket(dist)
    for h in range(NSA_HEADS):
        val = jnp.zeros((8, span), F32)
        for b in range(REL_BUCKETS):
            val = jnp.where(bucket == b, rel_ref[b, h], val)
        if sub_far:
            val = val - rel_ref[REL_BUCKETS - 1, h]
        val = jnp.where(valid, val * LOG2E, NEG)
        rows = jnp.broadcast_to(val[0:1, :], (CHUNK, span))
        tile = pltpu.roll(rows, span - (CHUNK - 1), 1, stride=1, stride_axis=0)[:, 0:CHUNK]
        hh = h % NSA_HPG
        out_ref[0, h // NSA_HPG, :, hh * CHUNK:(hh + 1) * CHUNK] = tile


def _cmp_bias_table(rel, seq):
    return pl.pallas_call(
        functools.partial(_bias_kernel, rows=N_CMP_PAD, cols=CMP_TQ, row_coef=-CMP_STRIDE,
                          offset=-(CMP_LEN - 1)),
        grid=(seq // CMP_TQ,),
        in_specs=[pl.BlockSpec(memory_space=pltpu.SMEM)],
        out_specs=pl.BlockSpec((NSA_HEADS, N_CMP_PAD, CMP_TQ), lambda m: (0, 0, m)),
        out_shape=jax.ShapeDtypeStruct((NSA_HEADS, N_CMP_PAD, seq), F32),
        compiler_params=_cparams(1),
        name="bias_cmp",
    )(rel)


def _chunk_bias_table(rel, *, limit, sub_far, n_tiles):
    return pl.pallas_call(
        functools.partial(_toeplitz_bias_kernel, limit=limit, sub_far=sub_far),
        grid=(n_tiles,),
        in_specs=[pl.BlockSpec(memory_space=pltpu.SMEM)],
        out_specs=pl.BlockSpec((1, NSA_GROUPS, CHUNK, NSA_HPG * CHUNK), lambda m: (m, 0, 0, 0)),
        out_shape=jax.ShapeDtypeStruct((n_tiles, NSA_GROUPS, CHUNK, NSA_HPG * CHUNK), F32),
        compiler_params=_cparams(1),
        name="bias_chunk",
    )(rel)


def _ada_kernel(c_ref, w_ref, b_ref, o_ref):
    c = c_ref[...]
    cond = c * _sigmoid(c)
    o_ref[...] = jnp.dot(cond, w_ref[...], precision=HIGHEST,
                         preferred_element_type=F32) + b_ref[...]


def _ada(c, w, b):
    bsz = c.shape[0]
    n = w.shape[1]
    tn = 1024
    return pl.pallas_call(
        _ada_kernel,
        grid=(n // tn,),
        in_specs=[pl.BlockSpec((bsz, D_MODEL), lambda j: (0, 0)),
                  pl.BlockSpec((D_MODEL, tn), lambda j: (0, j)),
                  pl.BlockSpec((1, tn), lambda j: (0, j))],
        out_specs=pl.BlockSpec((bsz, tn), lambda j: (0, j)),
        out_shape=jax.ShapeDtypeStruct((bsz, n), F32),
        compiler_params=_cparams(1),
        name="ada_mod",
    )(c, w, b.reshape(1, n))


def _modulated_norm(x, gain, shift, scale):
    ms = jnp.mean(x * x, axis=-1, keepdims=True)
    return (x * lax.rsqrt(ms + RMS_EPS) * gain) * (1.0 + scale) + shift


def _inproj_kernel(x_ref, mod_ref, g_ref, w_ref, wq_ref, z_ref, qt_ref, cmp_ref, *, tn):
    mod = mod_ref[0]
    h = _modulated_norm(x_ref[0], g_ref[...], mod[:, 0:D_MODEL],
                        mod[:, D_MODEL:2 * D_MODEL])
    hb = h.astype(BF16)
    tm = hb.shape[0]
    qt = lax.dot_general(wq_ref[...], hb, _NT, preferred_element_type=F32)
    zeros = jnp.zeros((LANES - NSA_D, tm), BF16)
    for hd in range(NSA_HEADS):
        qt_ref[0, hd * LANES:hd * LANES + NSA_D, :] = (
            qt[hd * NSA_D:(hd + 1) * NSA_D, :].astype(BF16))
        qt_ref[0, hd * LANES + NSA_D:(hd + 1) * LANES, :] = zeros
    for n0 in range(0, Z_WIDTH, tn):
        z_ref[0, :, n0:n0 + tn] = jnp.dot(
            hb, w_ref[:, n0:n0 + tn], preferred_element_type=F32).astype(BF16)
    acc = jnp.dot(hb, w_ref[:, Z_WIDTH:Z_WIDTH + CMP_WIDTH], preferred_element_type=F32)
    for g in range(NSA_GROUPS):
        cmp_ref[0, g] = acc[:, g * LANES:(g + 1) * LANES]


def _inproj(x, mod3, gain, w_p, wq_t, tm=512):
    bsz, seq, _ = x.shape
    return pl.pallas_call(
        functools.partial(_inproj_kernel, tn=512),
        grid=(bsz, seq // tm),
        in_specs=[pl.BlockSpec((1, tm, D_MODEL), lambda b, i: (b, i, 0)),
                  pl.BlockSpec((1, 1, 6 * D_MODEL), lambda b, i: (b, 0, 0)),
                  pl.BlockSpec((1, D_MODEL), lambda b, i: (0, 0)),
                  _const_spec(w_p.shape), _const_spec(wq_t.shape)],
        out_specs=[pl.BlockSpec((1, tm, Z_WIDTH), lambda b, i: (b, i, 0)),
                   pl.BlockSpec((1, QT_ROWS, tm), lambda b, i: (b, 0, i)),
                   pl.BlockSpec((1, NSA_GROUPS, tm, LANES), lambda b, i: (b, 0, i, 0))],
        out_shape=[jax.ShapeDtypeStruct((bsz, seq, Z_WIDTH), BF16),
                   jax.ShapeDtypeStruct((bsz, QT_ROWS, seq), BF16),
                   jax.ShapeDtypeStruct((bsz, NSA_GROUPS, seq, LANES), F32)],
        compiler_params=_cparams(2),
        name="in_proj",
    )(x, mod3, gain, w_p, wq_t)


def _gelu_tanh(x):
    return 0.5 * x * (1.0 + jnp.tanh(math.sqrt(2.0 / math.pi) * (x + 0.044715 * (x * x * x))))


def _pos_term_kernel(pos_ref, w1_ref, o_ref):
    pos = jnp.broadcast_to(pos_ref[0], (8, CMP_LEN * NSA_D))
    o_ref[0] = jnp.dot(pos, w1_ref[0], precision=HIGHEST, preferred_element_type=F32)


def _pos_term(pos, w1):
    return pl.pallas_call(
        _pos_term_kernel,
        grid=(2,),
        in_specs=[pl.BlockSpec((1, 1, CMP_LEN * NSA_D), lambda j: (j, 0, 0)),
                  pl.BlockSpec((1, CMP_LEN * NSA_D, CMP_HIDDEN), lambda j: (j, 0, 0))],
        out_specs=pl.BlockSpec((1, 8, CMP_HIDDEN), lambda j: (j, 0, 0)),
        out_shape=jax.ShapeDtypeStruct((2, 8, CMP_HIDDEN), F32),
        compiler_params=_cparams(1),
        name="cmp_pos_term",
    )(pos, w1)


def _compress_kernel(y_ref, pterm_ref, w1_ref, w2_ref, w2t_ref, kc_ref, vct_ref):
    nhb = y_ref.shape[2] // CMP_STRIDE
    low = lax.broadcasted_iota(jnp.int32, (nhb, LANES), 1) < NSA_D
    k_rows, v_rows = [], []
    for g in range(NSA_GROUPS):
        k_cols, v_cols = [], []
        for r in range(0, CMP_STRIDE, 2):
            a = y_ref[0, g, pl.ds(r, nhb, stride=CMP_STRIDE), :]
            b = y_ref[0, g, pl.ds(r + 1, nhb, stride=CMP_STRIDE), :]
            k_cols.append(jnp.where(low, a, pltpu.roll(b, NSA_D, 1)))
            v_cols.append(jnp.where(low, pltpu.roll(a, NSA_D, 1), b))
        k_rows.append(jnp.concatenate(k_cols, axis=1).astype(BF16))
        v_rows.append(jnp.concatenate(v_cols, axis=1).astype(BF16))
    half = (CMP_LEN // 2) * NSA_D
    outs = []
    for kv, rows in enumerate((k_rows, v_rows)):
        y = jnp.concatenate(rows, axis=0)
        top = jnp.dot(y, w1_ref[kv, 0:half, :], preferred_element_type=F32)
        bot = jnp.dot(y, w1_ref[kv, half:2 * half, :], preferred_element_type=F32)
        hidden = top + pltpu.roll(bot, NSA_GROUPS * nhb - 1, 0) + pterm_ref[kv, 0:1, :]
        outs.append(_gelu_tanh(hidden).astype(BF16))
    out = jnp.dot(outs[0], w2_ref[0], preferred_element_type=F32).astype(BF16)
    out_t = lax.dot_general(w2t_ref[1], outs[1], _NT,
                            preferred_element_type=F32).astype(BF16)
    for g in range(NSA_GROUPS):
        og = out[g * nhb:(g + 1) * nhb]
        kc_ref[0, g] = jnp.concatenate([og, jnp.zeros_like(og)], axis=1)
        vct_ref[0, g] = out_t[:, g * nhb:(g + 1) * nhb]


def _compress(cmp_in, pos, w1, w2):
    bsz, ng, seq, _ = cmp_in.shape
    nhb = seq // CMP_STRIDE
    w2b = w2.astype(BF16)
    return pl.pallas_call(
        _compress_kernel,
        grid=(bsz,),
        in_specs=[pl.BlockSpec((1, ng, seq, LANES), lambda b: (b, 0, 0, 0)),
                  _const_spec((2, 8, CMP_HIDDEN)),
                  _const_spec((2, CMP_LEN * NSA_D, CMP_HIDDEN)),
                  _const_spec((2, CMP_HIDDEN, NSA_D)),
                  _const_spec((2, NSA_D, CMP_HIDDEN))],
        out_specs=[pl.BlockSpec((1, ng, nhb, LANES), lambda b: (b, 0, 0, 0)),
                   pl.BlockSpec((1, ng, NSA_D, nhb), lambda b: (b, 0, 0, 0))],
        out_shape=[jax.ShapeDtypeStruct((bsz, ng, nhb, LANES), BF16),
                   jax.ShapeDtypeStruct((bsz, ng, NSA_D, nhb), BF16)],
        compiler_params=_cparams(1),
        name="compress",
    )(cmp_in, _pos_term(pos, w1), w1.astype(BF16), w2b, w2b.transpose(0, 2, 1))


def _select_mask(score, jj, cur):
    forced = (jj == 0) | (jj == cur) | (jj == cur - 1)
    x = jnp.where(forced, jnp.inf, jnp.where(jj > cur, -jnp.inf, score))
    below = pltpu.bitcast(pltpu.bitcast(x, jnp.int32) - 1, F32)
    x_lo = jnp.where(x > 0.0, below, jnp.where(x == 0.0, -1.0, x))
    tq = x.shape[1]
    n_sub = 8
    masks = []
    for v in range(N_SEL_BLOCKS // n_sub):
        rows = slice(v * n_sub, (v + 1) * n_sub)
        xv, xlv = x[rows], x_lo[rows]
        jv = v * n_sub + lax.broadcasted_iota(jnp.int32, (n_sub, tq), 0)
        rank = jnp.zeros((n_sub, tq), F32)
        for j2 in range(N_SEL_BLOCKS):
            row = x[j2:j2 + 1, :]
            if v * n_sub > j2:
                thr = xlv
            elif (v + 1) * n_sub - 1 <= j2:
                thr = xv
            else:
                thr = jnp.where(jv > j2, xlv, xv)
            rank = rank + jnp.where(row > thr, 1.0, 0.0)
        masks.append(jnp.where(rank < SEL_TOPK, 0.0, NEG))
    return jnp.concatenate(masks, axis=0)


def _cmp_attn_kernel(q_ref, misc_ref, kc_ref, vct_ref, bias_ref, ovl_ref, o_ref, mb_ref,
                     gt_scr, ot_scr):
    i = pl.program_id(1)
    tq = CMP_TQ
    gt_scr[...] = _sigmoid(misc_ref[0].astype(F32)).T
    jj = lax.broadcasted_iota(jnp.int32, (N_SEL_BLOCKS, tq), 0)
    tt = i * tq + lax.broadcasted_iota(jnp.int32, (N_SEL_BLOCKS, tq), 1)
    cur = tt // SEL_BLOCK
    t_row = i * tq + lax.broadcasted_iota(jnp.int32, (1, tq), 1)
    sees_any = t_row >= CMP_LEN - 1
    logits = [jnp.dot(kc_ref[0, h // NSA_HPG], q_ref[0, h * LANES:(h + 1) * LANES, :],
                      preferred_element_type=F32) + bias_ref[h]
              for h in range(NSA_HEADS)]
    for g in range(NSA_GROUPS):
        vct = vct_ref[0, g]
        psum = jnp.zeros((N_CMP_PAD, tq), F32)
        for hh in range(NSA_HPG):
            h = g * NSA_HPG + hh
            s = logits[h]
            e = jnp.exp2(s - jnp.max(s, axis=0, keepdims=True))
            inv = jnp.where(sees_any, 1.0 / jnp.sum(e, axis=0, keepdims=True), 0.0)
            p = e * inv
            psum = psum + p
            o = jnp.dot(vct, p.astype(BF16), preferred_element_type=F32)
            lane = GATE_LANE0 + 3 * h
            ot_scr[h * NSA_D:(h + 1) * NSA_D, :] = o * gt_scr[lane:lane + 1, :]
        score = jnp.dot(ovl_ref[...], psum, precision=HIGHEST,
                        preferred_element_type=F32)
        mb_ref[0, g * LANES:(g + 1) * LANES, :] = jnp.concatenate(
            [jnp.zeros((MASK_ROW0, tq), F32), _select_mask(score, jj, cur),
             jnp.zeros((LANES - MASK_ROW0 - N_SEL_BLOCKS, tq), F32)], axis=0).astype(BF16)
    o_ref[0] = ot_scr[...].T.astype(BF16)


def _cmp_attn(z, qt, kc, vct, bias_c, ovl):
    bsz, seq, _ = z.shape
    tq = CMP_TQ
    return pl.pallas_call(
        _cmp_attn_kernel,
        grid=(bsz, seq // tq),
        in_specs=[pl.BlockSpec((1, QT_ROWS, tq), lambda b, i: (b, 0, i)),
                  pl.BlockSpec((1, tq, LANES), lambda b, i: (b, i, ZC_MISC // LANES)),
                  pl.BlockSpec((1, NSA_GROUPS, N_CMP_PAD, LANES), lambda b, i: (b, 0, 0, 0)),
                  pl.BlockSpec((1, NSA_GROUPS, NSA_D, N_CMP_PAD), lambda b, i: (b, 0, 0, 0)),
                  pl.BlockSpec((NSA_HEADS, N_CMP_PAD, tq), lambda b, i: (0, 0, i)),
                  pl.BlockSpec((N_SEL_BLOCKS, N_CMP_PAD), lambda b, i: (0, 0))],
        out_specs=[pl.BlockSpec((1, tq, D_MODEL), lambda b, i: (b, i, 0)),
                   pl.BlockSpec((1, NSA_GROUPS * LANES, tq), lambda b, i: (b, 0, i))],
        out_shape=[jax.ShapeDtypeStruct((bsz, seq, D_MODEL), BF16),
                   jax.ShapeDtypeStruct((bsz, NSA_GROUPS * LANES, seq), BF16)],
        scratch_shapes=[pltpu.VMEM((LANES, tq), F32),
                        pltpu.VMEM((D_MODEL, tq), F32)],
        compiler_params=_cparams(2),
        name="cmp_attn_select",
    )(qt, z, kc, vct, bias_c, ovl)


def _flash_produce(s_ref, mx_ref, s):
    s_ref[...] = s
    mx_ref[...] = jnp.max(s, axis=0, keepdims=True)


def _flash_consume(s_ref, mx_ref, vt, m_ref, acc_ref):
    m_old = m_ref[...]
    m_new = jnp.maximum(m_old, mx_ref[...])
    alpha = jnp.exp2(m_old - m_new)
    p = jnp.exp2((s_ref[...] - m_new).astype(BF16))
    acc_ref[...] = alpha * acc_ref[...] + jnp.dot(vt, p, preferred_element_type=F32)
    m_ref[...] = m_new


def _flash_reset(m_ref, acc_ref):
    m_ref[...] = jnp.full(m_ref.shape, NEG, F32)
    acc_ref[...] = jnp.zeros(acc_ref.shape, F32)


def _ones_rows(n_cols):
    row = lax.broadcasted_iota(jnp.int32, (ONES_ROWS, n_cols), 0)
    return jnp.where(row == 0, 1.0, 0.0).astype(BF16)


def _run_flash(lo, hi, streams, produce_one, consume_one, common_n=None):
    n = hi - lo + 1

    def produce(c, slot, kind):
        for s in streams:
            produce_one(s, c, slot, kind)

    def consume(c, slot):
        for s in streams:
            consume_one(s, c, slot)

    def both(cp, sp, kind, cc, sc):
        order = list(streams)
        produce_one(order[0], cp, sp, kind)
        for prev, s in zip(order[:-1], order[1:]):
            produce_one(s, cp, sp, kind)
            consume_one(prev, cc, sc)
        consume_one(order[-1], cc, sc)

    def generic():
        produce(lo, 0, "first")

        @pl.when(n % 2 == 1)
        def _():
            nxt = jnp.minimum(lo + 1, hi)
            for s in streams:
                consume_one(s, lo, 0)
                produce_one(s, nxt, 0, "first")

        start = lo + n % 2
        n_pairs = n // 2

        def pair(t, carry):
            c = start + 2 * t
            both(c + 1, 1, "far", c, 0)
            both(c + 2, 0, "trail", c + 1, 1)
            return carry

        lax.fori_loop(0, n_pairs - 1, pair, 0)

        @pl.when(n_pairs >= 1)
        def _():
            both(hi, 1, "last", hi - 1, 0)
            consume(hi, 1)

    if common_n is None:
        generic()
        return

    @pl.when(n == common_n)
    def _():
        produce(lo, 0, "first")
        for k in range(common_n - 1):
            both(lo + k + 1, (k + 1) % 2, "last" if k + 2 == common_n else "first",
                 lo + k, k % 2)
        consume(hi, (common_n - 1) % 2)

    @pl.when(n != common_n)
    def _():
        generic()


def _nsa_flash_kernel(*refs, use_mask, n_back, gate_idx):
    if use_mask:
        q_ref, mb_ref, *refs = refs
    else:
        q_ref, *refs = refs
    misc_ref, k_ref, v_ref, bias_ref, o_ref, kaug, vt, qs_scr, gt_scr, ot_scr, *flash_scr = refs
    i = pl.program_id(1)
    seq = k_ref.shape[1]

    @pl.when(i == 0)
    def _():
        if use_mask:
            blk = lax.broadcasted_iota(jnp.int32, (seq, NSA_D), 0) // SEL_BLOCK
            lane = lax.broadcasted_iota(jnp.int32, (seq, NSA_D), 1)
            extra = jnp.where(blk == lane, 1.0, 0.0).astype(BF16)
        else:
            extra = jnp.zeros((seq, NSA_D), BF16)
        eye = (lax.broadcasted_iota(jnp.int32, (NSA_D, NSA_D), 0)
               == lax.broadcasted_iota(jnp.int32, (NSA_D, NSA_D), 1)).astype(BF16)
        for g in range(NSA_GROUPS):
            gs = slice(g * NSA_D, (g + 1) * NSA_D)
            kaug[g] = jnp.concatenate([k_ref[0, :, gs], extra], axis=1)
            for c in range(seq // CHUNK):
                vt[g, c, 0:NSA_D, :] = lax.dot_general(
                    eye, v_ref[0, c * CHUNK:(c + 1) * CHUNK, gs], _NT,
                    preferred_element_type=F32).astype(BF16)
                vt[g, c, NSA_D:NSA_D + ONES_ROWS, :] = _ones_rows(CHUNK)

    gt_scr[...] = _sigmoid(misc_ref[0].astype(F32)).T
    for h in range(NSA_HEADS):
        g, hh = divmod(h, NSA_HPG)
        qh = q_ref[0, h * LANES:(h + 1) * LANES, :]
        if use_mask:
            qh = qh + mb_ref[0, g * LANES:(g + 1) * LANES, :]
        qs_scr[g, :, hh * CHUNK:(hh + 1) * CHUNK] = qh

    s_scr, mx_scr, m_scr, acc_scr = flash_scr
    heads = range(NSA_HEADS)

    def view(ref, h, *lead):
        g, hh = divmod(h, NSA_HPG)
        return ref.at[(*lead, g, slice(None), pl.ds(hh * CHUNK, CHUNK))]

    for h in heads:
        _flash_reset(view(m_scr, h), view(acc_scr, h))

    def produce(h, c, slot, kind):
        g = h // NSA_HPG
        off = pl.multiple_of(c * CHUNK, CHUNK)
        s = jnp.dot(kaug[g, pl.ds(off, CHUNK), :], view(qs_scr, h)[...],
                    preferred_element_type=F32)
        if kind == "last":
            s = s + view(bias_ref, h, 0)[...]
        elif kind != "far" or n_back is not None:
            s = s + view(bias_ref, h, jnp.minimum(i - c, 2))[...]
        _flash_produce(view(s_scr, h, slot), view(mx_scr, h, slot), s)

    def consume(h, c, slot):
        _flash_consume(view(s_scr, h, slot), view(mx_scr, h, slot), vt[h // NSA_HPG, c],
                       view(m_scr, h), view(acc_scr, h))

    if n_back is None:
        _run_flash(0, i, heads, produce, consume)
    else:
        _run_flash(jnp.maximum(i - n_back, 0), i, heads, produce, consume,
                   common_n=n_back + 1)
    for h in heads:
        acc = view(acc_scr, h)
        lane = GATE_LANE0 + 3 * h + gate_idx
        scale = gt_scr[lane:lane + 1, :] / acc[NSA_D:NSA_D + 1, :]
        ot_scr[h * NSA_D:(h + 1) * NSA_D, :] = acc[0:NSA_D, :] * scale
    o_ref[0] = ot_scr[...].T.astype(BF16)


def _nsa_flash(z, qt, mb, bias_t, *, k_col, v_col, n_back, gate_idx):
    bsz, seq, _ = z.shape
    use_mask = mb is not None
    wide = NSA_HPG * CHUNK
    in_specs = [pl.BlockSpec((1, QT_ROWS, CHUNK), lambda b, i: (b, 0, i))]
    args = [qt]
    if use_mask:
        in_specs.append(pl.BlockSpec((1, NSA_GROUPS * LANES, CHUNK), lambda b, i: (b, 0, i)))
        args.append(mb)
    in_specs += [pl.BlockSpec((1, CHUNK, LANES), lambda b, i: (b, i, ZC_MISC // LANES)),
                 pl.BlockSpec((1, seq, 256), lambda b, i: (b, 0, k_col // 256)),
                 pl.BlockSpec((1, seq, 256), lambda b, i: (b, 0, v_col // 256)),
                 _const_spec(bias_t.shape)]
    args += [z, z, z, bias_t]
    scratch = [pltpu.VMEM((NSA_GROUPS, seq, LANES), BF16),
               pltpu.VMEM((NSA_GROUPS, seq // CHUNK, NSA_D + ONES_ROWS, CHUNK), BF16),
               pltpu.VMEM((NSA_GROUPS, LANES, wide), BF16),
               pltpu.VMEM((LANES, CHUNK), F32),
               pltpu.VMEM((D_MODEL, CHUNK), F32),
               pltpu.VMEM((2, NSA_GROUPS, CHUNK, wide), F32),
               pltpu.VMEM((2, NSA_GROUPS, 1, wide), F32),
               pltpu.VMEM((NSA_GROUPS, 1, wide), F32),
               pltpu.VMEM((NSA_GROUPS, NSA_D + ONES_ROWS, wide), F32)]
    return pl.pallas_call(
        functools.partial(_nsa_flash_kernel, use_mask=use_mask, n_back=n_back,
                          gate_idx=gate_idx),
        grid=(bsz, seq // CHUNK),
        in_specs=in_specs,
        out_specs=pl.BlockSpec((1, CHUNK, D_MODEL), lambda b, i: (b, i, 0)),
        out_shape=jax.ShapeDtypeStruct((bsz, seq, D_MODEL), BF16),
        scratch_shapes=scratch,
        compiler_params=_cparams(2),
        name="slc_attn" if use_mask else "win_attn",
    )(*args)


def _rms(x, gain):
    ms = jnp.mean(x * x, axis=-1, keepdims=True)
    return x * lax.rsqrt(ms + RMS_EPS) * gain


def _mla_prep_kernel(cq_ref, ckv_ref, misc_ref, pos_ref, qg_ref, kvg_ref, wqa_ref, wqb_ref,
                     wk_ref, wvt_ref, pa_ref, pb_ref, frq_ref, q_out, k_out, vt_out):
    tm = cq_ref.shape[1]
    cqn = _rms(cq_ref[0].astype(F32), qg_ref[...]).astype(BF16)
    ckvn = _rms(ckv_ref[0].astype(F32), kvg_ref[...]).astype(BF16)
    half = MLA_ROPE // 2
    ang = frq_ref[...] * pos_ref[0].astype(F32)
    cos_t, sin_t = jnp.cos(ang), jnp.sin(ang)
    ones = jnp.ones((MLA_NOPE, tm), F32)
    zeros = jnp.zeros((LANES - MLA_NOPE - 2 * half, tm), F32)
    cos_rows = jnp.concatenate([ones, cos_t, cos_t, zeros], axis=0)
    sin_rows = jnp.concatenate([0.0 * ones, sin_t, sin_t, zeros], axis=0)
    cos, sin = cos_rows.T, sin_rows.T
    scale = (MLA_NOPE + MLA_ROPE) ** -0.5 * LOG2E
    qa = lax.dot_general(wqa_ref[...], cqn, _NT, preferred_element_type=F32)
    qb = lax.dot_general(wqb_ref[...], cqn, _NT, preferred_element_type=F32)
    misc = misc_ref[0]
    kr = (jnp.dot(misc, pa_ref[...], preferred_element_type=F32) * cos
          + jnp.dot(misc, pb_ref[...], preferred_element_type=F32) * sin)
    kn = jnp.dot(ckvn, wk_ref[...], preferred_element_type=F32)
    vt = lax.dot_general(wvt_ref[...], ckvn, _NT,
                         preferred_element_type=F32).astype(BF16)
    for h in range(MLA_HEADS):
        sl = slice(h * LANES, (h + 1) * LANES)
        q_out[0, h] = ((qa[sl, :] * cos_rows + qb[sl, :] * sin_rows) * scale).astype(BF16)
        k_out[0, h] = (kn[:, sl] + kr).astype(BF16)
        for cc in range(tm // CHUNK):
            vt_out[0, h, cc, 0:MLA_V, :] = vt[sl, cc * CHUNK:(cc + 1) * CHUNK]
            vt_out[0, h, cc, MLA_V:MLA_V + ONES_ROWS, :] = _ones_rows(CHUNK)


def _mla_prep(z, pos3, qg, kvg, wqa, wqb, wk, wvt, pa, pb, frq, tm=1024):
    bsz, seq, _ = z.shape
    full = lambda shape: pl.BlockSpec(shape, lambda b, i: (0,) * len(shape))
    qk_shape = jax.ShapeDtypeStruct((bsz, MLA_HEADS, seq, LANES), BF16)
    qk_spec = pl.BlockSpec((1, MLA_HEADS, tm, LANES), lambda b, i: (b, 0, i, 0))
    qt_shape = jax.ShapeDtypeStruct((bsz, MLA_HEADS, LANES, seq), BF16)
    qt_spec = pl.BlockSpec((1, MLA_HEADS, LANES, tm), lambda b, i: (b, 0, 0, i))
    return pl.pallas_call(
        _mla_prep_kernel,
        grid=(bsz, seq // tm),
        in_specs=[pl.BlockSpec((1, tm, MLA_Q_RANK), lambda b, i: (b, i, ZC_CQ // MLA_Q_RANK)),
                  pl.BlockSpec((1, tm, MLA_KV_RANK), lambda b, i: (b, i, ZC_CKV // MLA_KV_RANK)),
                  pl.BlockSpec((1, tm, LANES), lambda b, i: (b, i, ZC_MISC // LANES)),
                  pl.BlockSpec((1, 1, tm), lambda b, i: (b, 0, i)),
                  full((1, MLA_Q_RANK)), full((1, MLA_KV_RANK)),
                  full((MLA_HEADS * LANES, MLA_Q_RANK)), full((MLA_HEADS * LANES, MLA_Q_RANK)),
                  full((MLA_KV_RANK, MLA_HEADS * LANES)), full((MLA_HEADS * MLA_V, MLA_KV_RANK)),
                  full((LANES, LANES)), full((LANES, LANES)), full((MLA_ROPE // 2, 1))],
        out_specs=[qt_spec, qk_spec,
                   pl.BlockSpec((1, MLA_HEADS, tm // CHUNK, MLA_V + ONES_ROWS, CHUNK),
                                lambda b, i: (b, 0, i, 0, 0))],
        out_shape=[qt_shape, qk_shape,
                   jax.ShapeDtypeStruct((bsz, MLA_HEADS, seq // CHUNK, MLA_V + ONES_ROWS, CHUNK),
                                        BF16)],
        compiler_params=_cparams(2),
        name="mla_prep",
    )(z, z, z, pos3, qg, kvg, wqa, wqb, wk, wvt, pa, pb, frq)


def _mla_attn_kernel(q_ref, k_ref, vt_ref, o_ref, ot_scr, s_scr, mx_scr, m_scr, acc_scr):
    i = pl.program_id(1)
    causal = (lax.broadcasted_iota(jnp.int32, (CHUNK, CHUNK), 0)
              <= lax.broadcasted_iota(jnp.int32, (CHUNK, CHUNK), 1))

    heads = range(MLA_HEADS)
    for h in heads:
        _flash_reset(m_scr.at[h], acc_scr.at[h])

    def produce(h, c, slot, kind):
        off = pl.multiple_of(c * CHUNK, CHUNK)
        s = jnp.dot(k_ref[0, h, pl.ds(off, CHUNK), :], q_ref[0, h],
                    preferred_element_type=F32)
        if kind == "last":
            s = jnp.where(causal, s, NEG)
        elif kind == "first":
            s = jnp.where(jnp.logical_or(causal, c < i), s, NEG)
        _flash_produce(s_scr.at[slot, h], mx_scr.at[slot, h], s)

    def consume(h, c, slot):
        _flash_consume(s_scr.at[slot, h], mx_scr.at[slot, h], vt_ref[0, h, c],
                       m_scr.at[h], acc_scr.at[h])

    _run_flash(0, i, heads, produce, consume)
    for h in heads:
        ot_scr[h * MLA_V:(h + 1) * MLA_V, :] = (
            acc_scr[h, 0:MLA_V, :] / acc_scr[h, MLA_V:MLA_V + 1, :])
    o_ref[0] = ot_scr[...].T.astype(BF16)


def _mla_attn(qt, k, vt):
    bsz, nh, seq, _ = k.shape
    hpt = nh
    return pl.pallas_call(
        _mla_attn_kernel,
        grid=(bsz, seq // CHUNK),
        in_specs=[pl.BlockSpec((1, nh, LANES, CHUNK), lambda b, i: (b, 0, 0, i)),
                  pl.BlockSpec((1, nh, seq, LANES), lambda b, i: (b, 0, 0, 0)),
                  pl.BlockSpec((1, nh, seq // CHUNK, MLA_V + ONES_ROWS, CHUNK),
                               lambda b, i: (b, 0, 0, 0, 0))],
        out_specs=pl.BlockSpec((1, CHUNK, nh * MLA_V), lambda b, i: (b, i, 0)),
        out_shape=jax.ShapeDtypeStruct((bsz, seq, nh * MLA_V), BF16),
        scratch_shapes=[pltpu.VMEM((nh * MLA_V, CHUNK), F32),
                        pltpu.VMEM((2, hpt, CHUNK, CHUNK), F32),
                        pltpu.VMEM((2, hpt, 1, CHUNK), F32),
                        pltpu.VMEM((hpt, 1, CHUNK), F32),
                        pltpu.VMEM((hpt, MLA_V + ONES_ROWS, CHUNK), F32)],
        compiler_params=_cparams(2),
        name="mla_attn",
    )(qt, k, vt)


def _merge_ffn_kernel(x_ref, mod_ref, oc_ref, os_ref, ow_ref, om_ref, ma_ref, mb_ref, wo_ref,
                      g_ref, wg_ref, wu_ref, cw_ref, cb_ref, wd_ref, fg_ref,
                      o_ref, prev_ref, *, tm):
    i = pl.program_id(1)

    @pl.when(i == 0)
    def _():
        prev_ref[...] = jnp.zeros_like(prev_ref)

    mod = mod_ref[0]
    o_nsa = oc_ref[0].astype(F32) + os_ref[0].astype(F32) + ow_ref[0].astype(F32)
    y = (_sigmoid(ma_ref[0].astype(F32)) * o_nsa
         + _sigmoid(mb_ref[0].astype(F32)) * om_ref[0].astype(F32))
    a = jnp.dot(y.astype(BF16), wo_ref[...], preferred_element_type=F32)
    x1 = x_ref[0] + mod[:, 2 * D_MODEL:3 * D_MODEL] * a
    h2 = _modulated_norm(x1, g_ref[...], mod[:, 3 * D_MODEL:4 * D_MODEL],
                         mod[:, 4 * D_MODEL:5 * D_MODEL]).astype(BF16)
    chunks = [slice(f0, f1) for f0, f1 in zip(FFN_COL_SPLITS[:-1], FFN_COL_SPLITS[1:])]
    acc = jnp.zeros((tm, D_MODEL), F32)

    def finish(fs, gt, up, acc):
        row = lax.broadcasted_iota(jnp.int32, gt.shape, 0)
        p1 = prev_ref[7:8, fs]
        p2 = prev_ref[6:7, fs]
        g1 = jnp.where(row == 0, p1, pltpu.roll(gt, 1, 0))
        g2 = jnp.where(row == 0, p2, jnp.where(row == 1, p1, pltpu.roll(gt, 2, 0)))
        prev_ref[:, fs] = gt[tm - 8:tm, :]
        conv = cb_ref[:, fs] + cw_ref[0:1, fs] * g2 + cw_ref[1:2, fs] * g1 + cw_ref[2:3, fs] * gt
        act = (conv * _sigmoid(conv)) * up
        return acc + jnp.dot(act.astype(BF16), wd_ref[fs, :], preferred_element_type=F32)

    pending = None
    for fs in chunks:
        gt = jnp.dot(h2, wg_ref[:, fs], preferred_element_type=F32)
        up = jnp.dot(h2, wu_ref[:, fs], preferred_element_type=F32)
        if pending is not None:
            acc = finish(*pending, acc)
        pending = (fs, gt, up)
    acc = finish(*pending, acc)
    x2 = x1 + mod[:, 5 * D_MODEL:6 * D_MODEL] * acc
    ms = jnp.mean(x2 * x2, axis=-1, keepdims=True)
    o_ref[0] = x2 * lax.rsqrt(ms + RMS_EPS) * fg_ref[...]


def _merge_ffn(x, mod3, o_cmp, o_slc, o_win, o_mla, z, w_o, gain, wg, wu, cw, cb, wd, fg,
               tm=512):
    bsz, seq, _ = x.shape
    tok = lambda col: pl.BlockSpec((1, tm, D_MODEL), lambda b, i: (b, i, col))
    return pl.pallas_call(
        functools.partial(_merge_ffn_kernel, tm=tm),
        grid=(bsz, seq // tm),
        in_specs=[tok(0),
                  pl.BlockSpec((1, 1, 6 * D_MODEL), lambda b, i: (b, 0, 0)),
                  tok(0), tok(0), tok(0), tok(0),
                  tok(ZC_MA // D_MODEL), tok(ZC_MB // D_MODEL),
                  _const_spec((D_MODEL, D_MODEL)), _const_spec((1, D_MODEL)),
                  _const_spec((D_MODEL, D_FF)), _const_spec((D_MODEL, D_FF)),
                  _const_spec((3, D_FF)), _const_spec((1, D_FF)), _const_spec((D_FF, D_MODEL)),
                  _const_spec((1, D_MODEL))],
        out_specs=tok(0),
        out_shape=jax.ShapeDtypeStruct((bsz, seq, D_MODEL), F32),
        scratch_shapes=[pltpu.VMEM((8, D_FF), F32)],
        compiler_params=_cparams(2),
        name="merge_ffn",
    )(x, mod3, o_cmp, o_slc, o_win, o_mla, z, z, w_o, gain, wg, wu, cw, cb, wd, fg)


def _prep_w_in(w_in):
    off = {}
    o = 0
    for name, width in (("nsa_q", 1024), ("k_cmp", 256), ("v_cmp", 256), ("k_slc", 256),
                        ("v_slc", 256), ("k_win", 256), ("v_win", 256), ("nsa_gate", 48),
                        ("mla_cq", 256), ("mla_ckv", 128), ("mla_krope", 32),
                        ("merge_a", 1024), ("merge_b", 1024)):
        off[name] = (o, o + width)
        o += width
    col = lambda name: w_in[:, off[name][0]:off[name][1]]
    q = col("nsa_q") * (NSA_D ** -0.5 * LOG2E)
    pad = jnp.zeros((D_MODEL, LANES - MLA_ROPE - 3 * NSA_HEADS), w_in.dtype)
    kv_cmp = jnp.stack([col("k_cmp").reshape(D_MODEL, NSA_GROUPS, NSA_D),
                        col("v_cmp").reshape(D_MODEL, NSA_GROUPS, NSA_D)],
                       axis=2).reshape(D_MODEL, CMP_WIDTH)
    parts = [col("merge_a"), col("merge_b"), col("k_slc"), col("v_slc"), col("k_win"),
             col("v_win"), col("mla_cq"), col("mla_ckv"), col("mla_krope"), col("nsa_gate"),
             pad, kv_cmp]
    return jnp.concatenate(parts, axis=1).astype(BF16), q.T.astype(BF16)


def _prep_mla_weights(w_uq, w_ukv):
    half = MLA_ROPE // 2
    dq = MLA_NOPE + MLA_ROPE
    wq = w_uq.reshape(MLA_Q_RANK, MLA_HEADS, dq)
    nope, x1, x2 = wq[..., :MLA_NOPE], wq[..., MLA_NOPE:MLA_NOPE + half], wq[..., MLA_NOPE + half:]
    z32 = jnp.zeros((MLA_Q_RANK, MLA_HEADS, LANES - dq), w_uq.dtype)
    wqa = jnp.concatenate([nope, x1, x2, z32], axis=-1)
    wqb = jnp.concatenate([jnp.zeros_like(nope), -x2, x1, z32], axis=-1)
    wkv = w_ukv.reshape(MLA_KV_RANK, MLA_HEADS, MLA_NOPE + MLA_V)
    wk = jnp.concatenate([wkv[..., :MLA_NOPE],
                          jnp.zeros((MLA_KV_RANK, MLA_HEADS, LANES - MLA_NOPE), w_ukv.dtype)],
                         axis=-1)
    wv = wkv[..., MLA_NOPE:]
    flat = lambda w: w.reshape(w.shape[0], MLA_HEADS * LANES).astype(BF16)
    return flat(wqa).T, flat(wqb).T, flat(wk), flat(wv).T


def _rope_constants():
    half = MLA_ROPE // 2
    pa = np.zeros((LANES, LANES), np.float32)
    pb = np.zeros((LANES, LANES), np.float32)
    for j in range(half):
        pa[j, MLA_NOPE + j] = 1.0
        pa[half + j, MLA_NOPE + half + j] = 1.0
        pb[half + j, MLA_NOPE + j] = -1.0
        pb[j, MLA_NOPE + half + j] = 1.0
    inv_freq = ROPE_THETA ** (-jnp.arange(0, MLA_ROPE, 2, dtype=F32) / MLA_ROPE)
    return jnp.asarray(pa, BF16), jnp.asarray(pb, BF16), inv_freq.reshape(half, 1)


def _overlap(seq):
    nc = (seq - CMP_LEN) // CMP_STRIDE + 1
    nb = seq // SEL_BLOCK
    cs = np.arange(nc) * CMP_STRIDE
    bs = np.arange(nb) * SEL_BLOCK
    ov = np.clip(np.minimum(cs[:, None] + CMP_LEN, bs[None, :] + SEL_BLOCK)
                 - np.maximum(cs[:, None], bs[None, :]), 0, None) / CMP_LEN
    out = np.zeros((nb, N_CMP_PAD), np.float32)
    out[:, :nc] = ov.T
    return jnp.asarray(out)


def kernel(x, c, positions, rel_bias_table, ada_w, ada_b, norm_mix_g, w_in, cmp_pos_k, cmp_w1_k, cmp_w2_k, cmp_pos_v, cmp_w1_v, cmp_w2_v, mla_q_norm_g, mla_w_uq, mla_kv_norm_g, mla_w_ukv, w_o, norm_ffn_g, ffn_w_gate, ffn_w_up, ffn_conv_w, ffn_conv_b, ffn_w_down, final_norm_g):
    bsz, seq, _ = x.shape
    assert ada_w.shape[0] == 1 and seq == N_SEL_BLOCKS * SEL_BLOCK
    n_back = WINDOW // CHUNK

    bias_c = _cmp_bias_table(rel_bias_table, seq)
    bias_s = _chunk_bias_table(rel_bias_table, limit=None, sub_far=True, n_tiles=3)
    bias_w = _chunk_bias_table(rel_bias_table, limit=WINDOW, sub_far=False, n_tiles=3)

    mod3 = _ada(c, ada_w[0], ada_b[0]).reshape(bsz, 1, 6 * D_MODEL)
    z, qt, cmp_in = _inproj(x, mod3, norm_mix_g, *_prep_w_in(w_in[0]))

    kvc, kvct = _compress(cmp_in,
                          jnp.stack([cmp_pos_k[0].reshape(1, -1), cmp_pos_v[0].reshape(1, -1)]),
                          jnp.stack([cmp_w1_k[0], cmp_w1_v[0]]),
                          jnp.stack([cmp_w2_k[0], cmp_w2_v[0]]))

    o_cmp, mb = _cmp_attn(z, qt, kvc, kvct, bias_c, _overlap(seq))
    o_slc = _nsa_flash(z, qt, mb, bias_s, k_col=ZC_KS, v_col=ZC_VS, n_back=None, gate_idx=1)
    o_win = _nsa_flash(z, qt, None, bias_w, k_col=ZC_KW, v_col=ZC_VW, n_back=n_back,
                       gate_idx=2)

    wqa, wqb, wk, wvt = _prep_mla_weights(mla_w_uq[0], mla_w_ukv[0])
    pa, pb, frq = _rope_constants()
    q_m, k_m, vt_m = _mla_prep(z, positions.reshape(bsz, 1, seq), mla_q_norm_g, mla_kv_norm_g,
                               wqa, wqb, wk, wvt, pa, pb, frq)
    o_mla = _mla_attn(q_m, k_m, vt_m)

    return _merge_ffn(x, mod3, o_cmp, o_slc, o_win, o_mla, z, w_o[0].astype(BF16), norm_ffn_g,
                      ffn_w_gate[0].astype(BF16), ffn_w_up[0].astype(BF16),
                      ffn_conv_w[0], ffn_conv_b, ffn_w_down[0].astype(BF16),
                      final_norm_g.reshape(1, D_MODEL))
```
